```python
import math
import jax, jax.numpy as jnp
from jax import lax
import numpy as np

D_MODEL = 2048
BATCH = 8
SEQ = 2048
DEPTH = 2

N_EVEN = (DEPTH + 1) // 2
N_ODD = DEPTH // 2
EPS = 1e-6

HG_HEADS = 8
HG_DIM = 128
HG_WIDTH = HG_HEADS * HG_DIM
HG_CHUNK = 64

MB_HEADS = 8
MB_DIM = 128
MB_WIDTH = MB_HEADS * MB_DIM
MB_BLOCK = 256
MB_TOPK = 3
MB_Q_TILE = 8

IN0_SIZES = [HG_WIDTH, HG_WIDTH, HG_WIDTH, HG_WIDTH, MB_WIDTH, MB_WIDTH, MB_WIDTH]
IN0_COLS = sum(IN0_SIZES)
IN0_SPLITS = [int(s) for s in np.cumsum(IN0_SIZES)[:-1]]
MIX0_WIDTH = HG_WIDTH + MB_WIDTH

SSM_INNER = 2 * D_MODEL
SSM_HEADDIM = 64
SSM_HEADS = SSM_INNER // SSM_HEADDIM
SSM_STATE = 128
SSM_GROUPS = 8
SSM_HPG = SSM_HEADS // SSM_GROUPS
SSM_CONV = 4
SSM_CHUNK = 256
SSM_CONV_CH = SSM_INNER + 2 * SSM_GROUPS * SSM_STATE
IN1_COLS = SSM_INNER + SSM_CONV_CH + SSM_HEADS

D_FF = -(-8 * D_MODEL // (3 * 256)) * 256

kernel_name = "hgrn2_moba_mamba2_hybrid"


def rms_norm(x, w):
    xf = x.astype(jnp.float32)
    y = xf * lax.rsqrt(jnp.mean(xf * xf, axis=-1, keepdims=True) + EPS)
    return (y * w.astype(jnp.float32)).astype(x.dtype)


def pad_seq(a, mult, axis):
    pad = (-a.shape[axis]) % mult
    if pad == 0:
        return a
    widths = [(0, 0)] * a.ndim
    widths[axis] = (0, pad)
    return jnp.pad(a, widths)


def hgrn2_mixer(q, f_raw, i, g, lb, norm_w):
    B_, S_, _ = q.shape
    f32 = jnp.float32
    q = jax.nn.silu(q.astype(f32))
    fr = f_raw.astype(f32)
    log_f = jnp.log(lb + (1.0 - lb) * jax.nn.sigmoid(fr))
    k = (1.0 - lb) * jax.nn.sigmoid(-fr)
    v = i.astype(f32)

    def to_chunks(a):
        a = pad_seq(a, HG_CHUNK, 1)
        nc = a.shape[1] // HG_CHUNK
        return a.reshape(B_, nc, HG_CHUNK, HG_HEADS, HG_DIM).transpose(1, 0, 3, 2, 4)

    qc, kc, vc, gc = to_chunks(q), to_chunks(k), to_chunks(v), to_chunks(log_f)
    causal = jnp.tril(jnp.ones((HG_CHUNK, HG_CHUNK), dtype=bool))

    def step(state, inp):
        qb, kb, vb, gb = inp
        b = jnp.cumsum(gb, axis=2)
        diff = b[:, :, :, None, :] - b[:, :, None, :, :]
        decay = jnp.exp(jnp.where(causal[:, :, None], diff, -jnp.inf))
        attn = jnp.einsum('bhtd,bhsd,bhtsd->bhts', qb, kb, decay)
        o_intra = jnp.einsum('bhts,bhsv->bhtv', attn, vb)
        o_inter = jnp.einsum('bhtd,bhdv->bhtv', qb * jnp.exp(b), state)
        b_last = b[:, :, -1]
        k_dec = kb * jnp.exp(b_last[:, :, None, :] - b)
        new_state = jnp.exp(b_last)[..., None] * state + jnp.einsum('bhsd,bhsv->bhdv', k_dec, vb)
        return new_state, o_intra + o_inter

    s0 = jnp.zeros((B_, HG_HEADS, HG_DIM, HG_DIM), f32)
    _, o = lax.scan(step, s0, (qc, kc, vc, gc))
    o = o.transpose(1, 0, 3, 2, 4).reshape(B_, -1, HG_HEADS, HG_DIM)[:, :S_]
    o = rms_norm(o, norm_w) * jax.nn.silu(g.astype(f32)).reshape(B_, S_, HG_HEADS, HG_DIM)
    return o.reshape(B_, S_, HG_WIDTH).astype(q.dtype)


def moba_mixer(q, k, v, q_norm_w, k_norm_w):
    B_, S_, _ = q.shape
    f32 = jnp.float32
    q = rms_norm(q.astype(f32).reshape(B_, S_, MB_HEADS, MB_DIM), q_norm_w).transpose(0, 2, 1, 3)
    k = rms_norm(k.astype(f32).reshape(B_, S_, MB_HEADS, MB_DIM), k_norm_w).transpose(0, 2, 1, 3)
    v = v.astype(f32).reshape(B_, S_, MB_HEADS, MB_DIM).transpose(0, 2, 1, 3)
    k_p = pad_seq(k, MB_BLOCK, 2)
    v_p = pad_seq(v, MB_BLOCK, 2)
    nb = k_p.shape[2] // MB_BLOCK
    top_k = min(MB_TOPK, nb)
    kb = k_p.reshape(B_, MB_HEADS, nb, MB_BLOCK, MB_DIM)
    vb = v_p.reshape(B_, MB_HEADS, nb, MB_BLOCK, MB_DIM)
    k_mean = jnp.mean(kb, axis=3)
    scale = MB_DIM ** -0.5
    bi = jnp.arange(B_)[:, None, None, None]
    hi = jnp.arange(MB_HEADS)[None, :, None, None]
    blk_ids = jnp.arange(nb)

    def tile(start):
        qt = lax.dynamic_slice_in_dim(q, start, MB_Q_TILE, axis=2)
        j = start // MB_BLOCK
        gate = jnp.einsum('bhtd,bhnd->bhtn', qt, k_mean)
        gate = jnp.where(blk_ids < j, gate, -jnp.inf)
        _, idx = lax.top_k(gate, top_k)
        valid = jnp.arange(top_k) < j
        kg = kb[bi, hi, idx]
        vg = vb[bi, hi, idx]
        s_sel = jnp.einsum('bhtd,bhtkjd->bhtkj', qt, kg) * scale
        s_sel = jnp.where(valid[:, None], s_sel, -jnp.inf).reshape(B_, MB_HEADS, MB_Q_TILE, top_k * MB_BLOCK)
        k_own = lax.dynamic_slice_in_dim(k_p, j * MB_BLOCK, MB_BLOCK, axis=2)
        v_own = lax.dynamic_slice_in_dim(v_p, j * MB_BLOCK, MB_BLOCK, axis=2)
        s_own = jnp.einsum('bhtd,bhjd->bhtj', qt, k_own) * scale
        q_pos = start + jnp.arange(MB_Q_TILE)
        k_pos = j * MB_BLOCK + jnp.arange(MB_BLOCK)
        s_own = jnp.where(k_pos[None, :] <= q_pos[:, None], s_own, -jnp.inf)
        p = jax.nn.softmax(jnp.concatenate([s_sel, s_own], axis=-1), axis=-1)
        p_sel = p[..., : top_k * MB_BLOCK].reshape(B_, MB_HEADS, MB_Q_TILE, top_k, MB_BLOCK)
        p_own = p[..., top_k * MB_BLOCK:]
        return (jnp.einsum('bhtkj,bhtkjd->bhtd', p_sel, vg)
                + jnp.einsum('bhtj,bhjd->bhtd', p_own, v_own))

    starts = jnp.arange(0, S_, MB_Q_TILE)
    out = lax.map(tile, starts)
    out = out.transpose(1, 0, 3, 2, 4).reshape(B_, S_, MB_WIDTH)
    return out


def ssd_scan(x, dt, A, Bm, Cm):
    B_, S_ = x.shape[:2]
    xdt = pad_seq(x * dt[..., None], SSM_CHUNK, 1)
    a = pad_seq(dt * A, SSM_CHUNK, 1)
    Bp = pad_seq(Bm, SSM_CHUNK, 1)
    Cp = pad_seq(Cm, SSM_CHUNK, 1)
    nc = xdt.shape[1] // SSM_CHUNK
    L = SSM_CHUNK
    xdt = jnp.moveaxis(xdt.reshape(B_, nc, L, SSM_GROUPS, SSM_HPG, SSM_HEADDIM), 1, 0)
    a = jnp.moveaxis(a.reshape(B_, nc, L, SSM_GROUPS, SSM_HPG), 1, 0)
    Bp = jnp.moveaxis(Bp.reshape(B_, nc, L, SSM_GROUPS, SSM_STATE), 1, 0)
    Cp = jnp.moveaxis(Cp.reshape(B_, nc, L, SSM_GROUPS, SSM_STATE), 1, 0)
    causal = jnp.tril(jnp.ones((L, L), dtype=bool))

    def step(state, inp):
        xc, ac, Bc, Cc = inp
        a_cs = jnp.cumsum(ac.transpose(0, 2, 3, 1), axis=-1)
        seg = a_cs[..., :, None] - a_cs[..., None, :]
        Lmat = jnp.exp(jnp.where(causal, seg, -jnp.inf))
        cb = jnp.einsum('btgn,bsgn->bgts', Cc, Bc)
        y_intra = jnp.einsum('bgrts,bsgrp->btgrp', cb[:, :, None] * Lmat, xc)
        y_inter = jnp.einsum('btgn,bgrpn,bgrt->btgrp', Cc, state, jnp.exp(a_cs))
        a_last = a_cs[..., -1]
        dec = jnp.exp(a_last[..., None] - a_cs)
        new_state = (jnp.exp(a_last)[..., None, None] * state
                     + jnp.einsum('bgrs,bsgrp,bsgn->bgrpn', dec, xc, Bc))
        return new_state, y_intra + y_inter

    s0 = jnp.zeros((B_, SSM_GROUPS, SSM_HPG, SSM_HEADDIM, SSM_STATE), jnp.float32)
    _, y = lax.scan(step, s0, (xdt, a, Bp, Cp))
    y = jnp.moveaxis(y, 0, 1).reshape(B_, nc * L, SSM_HEADS, SSM_HEADDIM)[:, :S_]
    return y


def mamba2_mixer(h, w_in, conv_w, conv_b, dt_bias, a_log, d_skip, norm_w, w_out):
    B_, S_, _ = h.shape
    f32 = jnp.float32
    zxbcdt = h @ w_in
    z, xbc, dt = jnp.split(zxbcdt, [SSM_INNER, SSM_INNER + SSM_CONV_CH], axis=-1)
    xbc = lax.conv_general_dilated(
        xbc, conv_w.reshape(SSM_CONV, 1, SSM_CONV_CH), window_strides=(1,),
        padding=[(SSM_CONV - 1, 0)], dimension_numbers=('NWC', 'WIO', 'NWC'),
        feature_group_count=SSM_CONV_CH)
    xbc = jax.nn.silu((xbc + conv_b).astype(f32))
    xs, Bm, Cm = jnp.split(xbc, [SSM_INNER, SSM_INNER + SSM_GROUPS * SSM_STATE], axis=-1)
    xs = xs.reshape(B_, S_, SSM_HEADS, SSM_HEADDIM)
    Bm = Bm.reshape(B_, S_, SSM_GROUPS, SSM_STATE)
    Cm = Cm.reshape(B_, S_, SSM_GROUPS, SSM_STATE)
    dt = jax.nn.softplus(dt.astype(f32) + dt_bias.astype(f32))
    A = -jnp.exp(a_log.astype(f32))
    y = ssd_scan(xs, dt, A, Bm, Cm) + d_skip.astype(f32)[:, None] * xs
    gy = (y.reshape(B_, S_, SSM_INNER) * jax.nn.silu(z.astype(f32))).reshape(B_, S_, SSM_GROUPS, -1)
    gy = gy * lax.rsqrt(jnp.mean(gy * gy, axis=-1, keepdims=True) + EPS)
    gy = (gy.reshape(B_, S_, SSM_INNER) * norm_w.astype(f32)).astype(h.dtype)
    return gy @ w_out


def swiglu_ffn(h, w_gate, w_up, w_down):
    return (jax.nn.silu(h @ w_gate) * (h @ w_up)) @ w_down


def setup_inputs(seed: int = 0) -> dict:
    key = jax.random.key(seed)
    ks = jax.random.split(key, 24)
    f32 = jnp.float32

    def nrm(k, shape, fan_in):
        return jax.random.normal(k, shape, f32) * fan_in ** -0.5

    def gain(k, shape):
        return 1.0 + 0.02 * jax.random.normal(k, shape, f32)

    dt0 = jnp.exp(jax.random.uniform(ks[15], (N_ODD, SSM_HEADS), f32,
                                     math.log(1e-3), math.log(1e-1)))
    return {
        "x": jax.random.normal(ks[0], (BATCH, SEQ, D_MODEL), f32),
        "ln_mix": gain(ks[1], (DEPTH, D_MODEL)),
        "ln_ffn": gain(ks[2], (DEPTH, D_MODEL)),
        "w_in_even": nrm(ks[3], (N_EVEN, D_MODEL, IN0_COLS), D_MODEL),
        "w_out_even": nrm(ks[4], (N_EVEN, MIX0_WIDTH, D_MODEL), MIX0_WIDTH),
        "hgrn_lb": 0.1 * jax.random.normal(ks[5], (N_EVEN + 1, HG_WIDTH), f32),
        "hgrn_norm": gain(ks[6], (N_EVEN, HG_DIM)),
        "q_norm": gain(ks[7], (N_EVEN, MB_DIM)),
        "k_norm": gain(ks[8], (N_EVEN, MB_DIM)),
        "w_in_ssm": nrm(ks[9], (N_ODD, D_MODEL, IN1_COLS), D_MODEL),
        "conv_w": nrm(ks[10], (N_ODD, SSM_CONV, SSM_CONV_CH), SSM_CONV),
        "conv_b": 0.02 * jax.random.normal(ks[11], (N_ODD, SSM_CONV_CH), f32),
        "dt_bias": dt0 + jnp.log(-jnp.expm1(-dt0)),
        "a_log": jnp.log(jax.random.uniform(ks[12], (N_ODD, SSM_HEADS), f32, 1.0, 16.0)),
        "d_skip": gain(ks[13], (N_ODD, SSM_HEADS)),
        "ssm_norm": gain(ks[14], (N_ODD, SSM_INNER)),
        "w_out_ssm": nrm(ks[16], (N_ODD, SSM_INNER, D_MODEL), SSM_INNER),
        "w_gate": nrm(ks[17], (DEPTH, D_MODEL, D_FF), D_MODEL),
        "w_up": nrm(ks[18], (DEPTH, D_MODEL, D_FF), D_MODEL),
        "w_down": nrm(ks[19], (DEPTH, D_FF, D_MODEL), D_FF),
    }


def reference(x, ln_mix, ln_ffn, w_in_even, w_out_even, hgrn_lb, hgrn_norm, q_norm, k_norm,
              w_in_ssm, conv_w, conv_b, dt_bias, a_log, d_skip, ssm_norm, w_out_ssm,
              w_gate, w_up, w_down):
    lb_all = jnp.cumsum(jax.nn.softmax(hgrn_lb.astype(jnp.float32), axis=0), axis=0)
    for layer in range(DEPTH):
        h = rms_norm(x, ln_mix[layer])
        if layer % 2 == 0:
            e = layer // 2
            proj = h @ w_in_even[e]
            hq, hf, hi, hg, mq, mk, mv = jnp.split(proj, IN0_SPLITS, axis=-1)
            o_a = hgrn2_mixer(hq, hf, hi, hg, lb_all[e], hgrn_norm[e])
            o_b = moba_mixer(mq, mk, mv, q_norm[e], k_norm[e]).astype(h.dtype)
            x = x + jnp.concatenate([o_a, o_b], axis=-1) @ w_out_even[e]
        else:
            o = layer // 2
            x = x + mamba2_mixer(h, w_in_ssm[o], conv_w[o], conv_b[o], dt_bias[o], a_log[o],
                                 d_skip[o], ssm_norm[o], w_out_ssm[o])
        h = rms_norm(x, ln_ffn[layer])
        x = x + swiglu_ffn(h, w_gate[layer], w_up[layer], w_down[layer])
    return x
```

```python
import functools
import math

import jax
import jax.numpy as jnp
from jax import lax
from jax.experimental import pallas as pl
from jax.experimental.pallas import tpu as pltpu

F32 = jnp.float32
BF16 = jnp.bfloat16
EPS = 1e-6
NEG_INF = float("-inf")

HG_DIM = 128
MB_DIM = 128
MB_BLOCK = 256
MB_TOPK = 3
SSM_HEADDIM = 64
SSM_STATE = 128
SSM_GROUPS = 8
SSM_CONV = 4

V7X_LANES = 128
V7X_SUBLANES = 8
V7X_VMEM_BYTES = 64 * 1024 * 1024

HGRN_CHUNK = 256
SSD_CHUNK = 256
NORM_ROWS = 128


def _cparams(semantics, vmem_bytes):
    return pltpu.CompilerParams(dimension_semantics=semantics, vmem_limit_bytes=int(vmem_bytes))


def _vmem_limit(estimate_bytes):
    return min(V7X_VMEM_BYTES - 6 * 1024 * 1024, max(32 * 1024 * 1024, int(estimate_bytes * 1.25)))


def _silu(x):
    return x * jax.nn.sigmoid(x)


def _rmsnorm_rows(x_ref, gain_ref, h_ref):
    n_steps = x_ref.shape[0] // NORM_ROWS

    def body(r, carry):
        rows = pl.ds(pl.multiple_of(r * NORM_ROWS, NORM_ROWS), NORM_ROWS)
        x = x_ref[rows, :]
        ms = jnp.mean(x * x, axis=-1, keepdims=True)
        h_ref[rows, :] = (x * lax.rsqrt(ms + EPS) * gain_ref[...]).astype(h_ref.dtype)
        return carry

    lax.fori_loop(0, n_steps, body, 0)


def _norm_matmul_kernel(x_ref, gain_ref, w_ref, o_ref, h_ref):
    @pl.when(pl.program_id(1) == 0)
    def _():
        _rmsnorm_rows(x_ref, gain_ref, h_ref)

    o_ref[...] = jnp.dot(h_ref[...], w_ref[...], preferred_element_type=F32).astype(o_ref.dtype)


def _norm_matmul_extra_kernel(x_ref, gain_ref, w_ref, w2_ref, o_ref, o2_ref, h_ref):
    @pl.when(pl.program_id(1) == 0)
    def _():
        _rmsnorm_rows(x_ref, gain_ref, h_ref)
        o2_ref[...] = jnp.dot(h_ref[...], w2_ref[...], preferred_element_type=F32)

    o_ref[...] = jnp.dot(h_ref[...], w_ref[...], preferred_element_type=F32).astype(o_ref.dtype)


def _norm_matmul(x, gain, w, w_extra=None, *, tm, tn):
    T, D = x.shape
    N = w.shape[1]
    grid = (T // tm, N // tn)
    in_specs = [
        pl.BlockSpec((tm, D), lambda i, j: (i, 0)),
        pl.BlockSpec((1, D), lambda i, j: (0, 0)),
        pl.BlockSpec((D, tn), lambda i, j: (0, j)),
    ]
    out_shape = [jax.ShapeDtypeStruct((T, N), F32)]
    out_specs = [pl.BlockSpec((tm, tn), lambda i, j: (i, j))]
    args = [x, gain.reshape(1, D), w]
    est = 2 * tm * D * 4 + tm * D * 2 + 2 * D * tn * 2 + 3 * tm * tn * 4
    if w_extra is None:
        body = _norm_matmul_kernel
    else:
        body = _norm_matmul_extra_kernel
        n2 = w_extra.shape[1]
        in_specs.append(pl.BlockSpec((D, n2), lambda i, j: (0, 0)))
        out_shape.append(jax.ShapeDtypeStruct((T, n2), F32))
        out_specs.append(pl.BlockSpec((tm, n2), lambda i, j: (i, 0)))
        args.append(w_extra)
        est += 2 * D * n2 * 2 + 2 * tm * n2 * 4
    res = pl.pallas_call(
        body,
        grid=grid,
        in_specs=in_specs,
        out_specs=out_specs,
        out_shape=out_shape,
        scratch_shapes=[pltpu.VMEM((tm, D), BF16)],
        compiler_params=_cparams(("parallel", "arbitrary"), _vmem_limit(est)),
        name="norm_in_proj",
    )(*args)
    return res[0] if w_extra is None else res


def _proj_residual_kernel(*refs, n_lhs):
    lhs_refs = refs[:n_lhs]
    w_ref, x_ref, o_ref = refs[n_lhs:]
    acc = x_ref[...]
    k0 = 0
    for a_ref in lhs_refs:
        k1 = k0 + a_ref.shape[1]
        acc = acc + jnp.dot(a_ref[...], w_ref[k0:k1, :], preferred_element_type=F32)
        k0 = k1
    o_ref[...] = acc


def _proj_residual(lhs_list, w, x, *, tm, tn):
    T, N = x.shape
    K = w.shape[0]
    grid = (T // tm, N // tn)
    in_specs = [pl.BlockSpec((tm, a.shape[1]), lambda i, j: (i, 0)) for a in lhs_list]
    in_specs += [
        pl.BlockSpec((K, tn), lambda i, j: (0, j)),
        pl.BlockSpec((tm, tn), lambda i, j: (i, j)),
    ]
    est = 2 * tm * K * 2 + 2 * K * tn * 2 + 5 * tm * tn * 4
    return pl.pallas_call(
        functools.partial(_proj_residual_kernel, n_lhs=len(lhs_list)),
        grid=grid,
        in_specs=in_specs,
        out_specs=pl.BlockSpec((tm, tn), lambda i, j: (i, j)),
        out_shape=jax.ShapeDtypeStruct((T, N), F32),
        compiler_params=_cparams(("parallel", "arbitrary"), _vmem_limit(est)),
        name="out_proj_residual",
    )(*lhs_list, w, x)


def _ffn_kernel(x_ref, gain_ref, wg_ref, wu_ref, wd_ref, o_ref, h_ref):
    f = pl.program_id(1)

    @pl.when(f == 0)
    def _():
        _rmsnorm_rows(x_ref, gain_ref, h_ref)

    h = h_ref[...]
    g = jnp.dot(h, wg_ref[...], preferred_element_type=F32)
    u = jnp.dot(h, wu_ref[...], preferred_element_type=F32)
    a = (_silu(g) * u).astype(BF16)
    d = jnp.dot(a, wd_ref[...], preferred_element_type=F32)

    @pl.when(f == 0)
    def _():
        o_ref[...] = x_ref[...] + d

    @pl.when(f != 0)
    def _():
        o_ref[...] += d


def _ffn(x, gain, wg, wu, wd, *, tm, tf):
    T, D = x.shape
    F = wg.shape[1]
    grid = (T // tm, F // tf)
    est = 4 * tm * D * 4 + tm * D * 2 + 3 * 2 * D * tf * 2 + 4 * tm * tf * 4 + tm * D * 4
    return pl.pallas_call(
        _ffn_kernel,
        grid=grid,
        in_specs=[
            pl.BlockSpec((tm, D), lambda i, f: (i, 0)),
            pl.BlockSpec((1, D), lambda i, f: (0, 0)),
            pl.BlockSpec((D, tf), lambda i, f: (0, f)),
            pl.BlockSpec((D, tf), lambda i, f: (0, f)),
            pl.BlockSpec((tf, D), lambda i, f: (f, 0)),
        ],
        out_specs=pl.BlockSpec((tm, D), lambda i, f: (i, 0)),
        out_shape=jax.ShapeDtypeStruct((T, D), F32),
        scratch_shapes=[pltpu.VMEM((tm, D), BF16)],
        compiler_params=_cparams(("parallel", "arbitrary"), _vmem_limit(est)),
        name="swiglu_ffn",
    )(x, gain.reshape(1, D), wg, wu, wd)


def _cumsum_rows(a, row):
    n = a.shape[0]
    shift = 1
    while shift < n:
        a = a + jnp.where(row >= shift, pltpu.roll(a, shift, 0), 0.0)
        shift *= 2
    return a


def _hgrn_kernel(q_ref, f_ref, i_ref, g_ref, lb_ref, nw_ref, o_ref, st_ref, *, slot):
    C, D = q_ref.shape

    @pl.when(pl.program_id(2) == 0)
    def _():
        st_ref[...] = jnp.zeros_like(st_ref)

    lbl = lb_ref[...]
    ex = jnp.exp(lbl - jnp.max(lbl, axis=0, keepdims=True))
    lb = jnp.sum(ex[: slot + 1], axis=0, keepdims=True) / jnp.sum(ex, axis=0, keepdims=True)

    fr = f_ref[...]
    q = _silu(q_ref[...])
    log_f = jnp.log(lb + (1.0 - lb) * jax.nn.sigmoid(fr))
    k = (1.0 - lb) * jax.nn.sigmoid(-fr)
    v = i_ref[...]

    row = lax.broadcasted_iota(jnp.int32, (C, D), 0)
    b = _cumsum_rows(log_f, row)

    tt = lax.broadcasted_iota(jnp.int32, (C, C), 0)
    ss = lax.broadcasted_iota(jnp.int32, (C, C), 1)
    differ = tt ^ ss
    attn = jnp.where(differ == 0, jnp.sum(q * k, axis=1, keepdims=True), 0.0)

    rolled = {}

    def shifted(o):
        if o not in rolled:
            rolled[o] = pltpu.roll(b, o % C, 0)
        return rolled[o]

    n_levels = int(math.log2(C))
    for lvl in range(n_levels):
        half = 1 << lvl
        blk = 2 * half
        if blk <= V7X_SUBLANES:
            off = (row & (blk - 1)) - (half - 1)
            beta = b
            for o in range(-(half - 1), half + 1):
                if o != 0:
                    beta = jnp.where(off == o, shifted(o), beta)
        else:
            mid = b.reshape(C // blk, blk, D)[:, half - 1:half, :]
            beta = jnp.broadcast_to(mid, (C // blk, blk, D)).reshape(C, D)
        is_q = (row & half) != 0
        e = jnp.exp(jnp.where(is_q, b - beta, beta - b))
        qt = jnp.where(is_q, q * e, 0.0).astype(BF16)
        kt = jnp.where(is_q, 0.0, k * e).astype(BF16)
        p = lax.dot_general(qt, kt, (((1,), (1,)), ((), ())), preferred_element_type=F32)
        if blk < C:
            p = jnp.where(differ < blk, p, 0.0)
        attn = attn + p

    st = st_ref[...]
    o = jnp.dot(attn.astype(BF16), v.astype(BF16), preferred_element_type=F32)
    qe = (q * jnp.exp(b)).astype(BF16)
    o = o + lax.dot_general(qe, st.astype(BF16), (((1,), (1,)), ((), ())), preferred_element_type=F32)

    b_last = b[C - 1:C, :]
    k_dec = (k * jnp.exp(b_last - b)).astype(BF16)
    v_t = v.T.astype(BF16)
    st_ref[...] = st * jnp.exp(b_last) + jnp.dot(v_t, k_dec, preferred_element_type=F32)

    ms = jnp.mean(o * o, axis=-1, keepdims=True)
    o = o * lax.rsqrt(ms + EPS) * nw_ref[...] * _silu(g_ref[...])
    o_ref[...] = o.astype(o_ref.dtype)


def _hgrn(proj, lb_table, norm_w, *, batch, seq, n_heads, col0, slot):
    T = proj.shape[0]
    C = HGRN_CHUNK
    nc = seq // C
    cb0 = col0 // HG_DIM
    n_slots = lb_table.shape[0]

    def sec(s):
        return pl.BlockSpec((C, HG_DIM), lambda b, h, c: (b * nc + c, cb0 + s * n_heads + h))

    return pl.pallas_call(
        functools.partial(_hgrn_kernel, slot=slot),
        grid=(batch, n_heads, nc),
        in_specs=[
            sec(0), sec(1), sec(2), sec(3),
            pl.BlockSpec((n_slots, HG_DIM), lambda b, h, c: (0, h)),
            pl.BlockSpec((1, HG_DIM), lambda b, h, c: (0, 0)),
        ],
        out_specs=pl.BlockSpec((C, HG_DIM), lambda b, h, c: (b * nc + c, h)),
        out_shape=jax.ShapeDtypeStruct((T, n_heads * HG_DIM), BF16),
        scratch_shapes=[pltpu.VMEM((HG_DIM, HG_DIM), F32)],
        compiler_params=_cparams(("parallel", "parallel", "arbitrary"), 32 * 1024 * 1024),
        name="hgrn2_scan",
    )(proj, proj, proj, proj, lb_table, norm_w.reshape(1, HG_DIM))


def _moba_kernel(q_ref, k_ref, v_ref, qw_ref, kw_ref, o_ref):
    S, D = q_ref.shape
    BLK = MB_BLOCK
    nb = S // BLK
    scale = D ** -0.5

    q = q_ref[...]
    k = k_ref[...]
    qn = q * lax.rsqrt(jnp.mean(q * q, axis=-1, keepdims=True) + EPS) * qw_ref[...]
    kn = k * lax.rsqrt(jnp.mean(k * k, axis=-1, keepdims=True) + EPS) * kw_ref[...]
    k_mean = jnp.mean(kn.reshape(nb, BLK, D), axis=1)
    qn_t = qn.T
    gate_t = jnp.dot(k_mean, qn_t, precision=lax.Precision.HIGHEST,
                     preferred_element_type=F32)
    qs_t = (qn_t * scale).astype(BF16)
    kn_b = kn.astype(BF16)
    v_t = v_ref[...].T.astype(BF16)

    kk = lax.broadcasted_iota(jnp.int32, (BLK, BLK), 0)
    qq = lax.broadcasted_iota(jnp.int32, (BLK, BLK), 1)
    causal = kk <= qq

    for j in range(nb):
        cols = slice(j * BLK, (j + 1) * BLK)
        q_j = qs_t[:, cols]
        n_keep = min(MB_TOPK, j)
        selected = None
        if j > n_keep:
            g = [gate_t[m:m + 1, cols] for m in range(j)]
            selected = []
            for n in range(j):
                beaten = jnp.zeros((1, BLK), jnp.int32)
                for m in range(j):
                    if m < n:
                        beaten = beaten + (g[m] >= g[n]).astype(jnp.int32)
                    elif m > n:
                        beaten = beaten + (g[m] > g[n]).astype(jnp.int32)
                selected.append(beaten < n_keep)
        scores = []
        for n in range(j + 1):
            s = jnp.dot(kn_b[n * BLK:(n + 1) * BLK, :], q_j, preferred_element_type=F32)
            if n == j:
                s = jnp.where(causal, s, NEG_INF)
            elif selected is not None:
                s = jnp.where(selected[n], s, NEG_INF)
            scores.append(s)
        m_run = jnp.max(scores[0], axis=0, keepdims=True)
        for s in scores[1:]:
            m_run = jnp.maximum(m_run, jnp.max(s, axis=0, keepdims=True))
        denom = jnp.zeros((1, BLK), F32)
        acc = jnp.zeros((D, BLK), F32)
        for n, s in enumerate(scores):
            p = jnp.exp(s - m_run)
            denom = denom + jnp.sum(p, axis=0, keepdims=True)
            acc = acc + jnp.dot(v_t[:, n * BLK:(n + 1) * BLK], p.astype(BF16), preferred_element_type=F32)
        o_ref[j * BLK:(j + 1) * BLK, :] = (acc / denom).T.astype(o_ref.dtype)


def _moba(proj, q_norm_w, k_norm_w, *, batch, seq, n_heads, col0):
    T = proj.shape[0]
    cb0 = col0 // MB_DIM

    def sec(s):
        return pl.BlockSpec((seq, MB_DIM), lambda b, h: (b, cb0 + s * n_heads + h))

    return pl.pallas_call(
        _moba_kernel,
        grid=(batch, n_heads),
        in_specs=[
            sec(0), sec(1), sec(2),
            pl.BlockSpec((1, MB_DIM), lambda b, h: (0, 0)),
            pl.BlockSpec((1, MB_DIM), lambda b, h: (0, 0)),
        ],
        out_specs=pl.BlockSpec((seq, MB_DIM), lambda b, h: (b, h)),
        out_shape=jax.ShapeDtypeStruct((T, n_heads * MB_DIM), BF16),
        compiler_params=_cparams(("parallel", "parallel"), 48 * 1024 * 1024),
        name="moba_attention",
    )(proj, proj, proj, q_norm_w.reshape(1, MB_DIM), k_norm_w.reshape(1, MB_DIM))


def _causal_conv_silu(raw, tail, w_ref, bias_ref):
    K = SSM_CONV
    acc = raw * w_ref[K - 1:K, :] + bias_ref[...]
    row8 = lax.broadcasted_iota(jnp.int32, tail.shape, 0)
    for d in range(1, K):
        sh = pltpu.roll(raw, d, 0)
        head = jnp.where(row8 < d, pltpu.roll(tail, d, 0), sh[:V7X_SUBLANES])
        sh = jnp.concatenate([head, sh[V7X_SUBLANES:]], axis=0)
        acc = acc + sh * w_ref[K - 1 - d:K - d, :]
    return _silu(acc)


def _expand_heads(cols, n_heads, width):
    rows = cols.shape[0]
    return jnp.concatenate(
        [jnp.broadcast_to(cols[:, r:r + 1], (rows, width)) for r in range(n_heads)], axis=1)


def _ssd_kernel(z_ref, x_ref, bm_ref, cm_ref, dt_ref, cwx_ref, cwb_ref, cwc_ref, cbx_ref, cbb_ref, cbc_ref,
                dtb_ref, alog_ref, dsk_ref, nw_ref, o_ref, st_ref, tail_ref):
    L, W = x_ref.shape
    P = SSM_HEADDIM
    hpg = W // P
    N = bm_ref.shape[1]
    g = pl.program_id(2)

    @pl.when(pl.program_id(1) == 0)
    def _():
        st_ref[g] = jnp.zeros(st_ref.shape[1:], F32)
        tail_ref[g] = jnp.zeros(tail_ref.shape[1:], F32)

    x_raw = x_ref[...]
    b_raw = bm_ref[...]
    c_raw = cm_ref[...]
    tail = tail_ref[g]
    xs = _causal_conv_silu(x_raw, tail[:, :W], cwx_ref, cbx_ref)
    bm = _causal_conv_silu(b_raw, tail[:, W:W + N], cwb_ref, cbb_ref)
    cm = _causal_conv_silu(c_raw, tail[:, W + N:], cwc_ref, cbc_ref)
    tail_ref[g] = jnp.concatenate(
        [x_raw[L - V7X_SUBLANES:], b_raw[L - V7X_SUBLANES:], c_raw[L - V7X_SUBLANES:]], axis=1)

    pre = dt_ref[...] + dtb_ref[...]
    dt = jnp.maximum(pre, 0.0) + jnp.log1p(jnp.exp(-jnp.abs(pre)))
    a = dt * (-jnp.exp(alog_ref[...]))
    row = lax.broadcasted_iota(jnp.int32, a.shape, 0)
    cs = _cumsum_rows(a, row)
    cs_t = cs.T
    dt_t = dt.T
    a_last = cs[L - 1:L, :]

    xs_b = xs.astype(BF16)
    bm_b = bm.astype(BF16)
    cm_b = cm.astype(BF16)
    cb = lax.dot_general(cm_b, bm_b, (((1,), (1,)), ((), ())), preferred_element_type=F32)
    tt = lax.broadcasted_iota(jnp.int32, (L, L), 0)
    ss = lax.broadcasted_iota(jnp.int32, (L, L), 1)
    causal = tt >= ss

    y_heads = []
    for r in range(hpg):
        seg = cs[:, r:r + 1] - cs_t[r:r + 1, :]
        lmat = jnp.exp(jnp.where(causal, seg, NEG_INF))
        gm = (cb * lmat * dt_t[r:r + 1, :]).astype(BF16)
        y_heads.append(jnp.dot(gm, xs_b[:, r * P:(r + 1) * P], preferred_element_type=F32))
    y = jnp.concatenate(y_heads, axis=1)

    st = st_ref[g]
    y = y + jnp.dot(cm_b, st.astype(BF16), preferred_element_type=F32) * _expand_heads(jnp.exp(cs), hpg, P)
    y = y + _expand_heads(dsk_ref[...], hpg, P) * xs

    w_in = _expand_heads(jnp.exp(a_last - cs) * dt, hpg, P)
    xw = (xs * w_in).astype(BF16)
    bm_t = bm.T.astype(BF16)
    st_ref[g] = st * _expand_heads(jnp.exp(a_last), hpg, P) + jnp.dot(bm_t, xw, preferred_element_type=F32)

    gy = y * _silu(z_ref[...])
    ms = jnp.mean(gy * gy, axis=-1, keepdims=True)
    o_ref[...] = (gy * lax.rsqrt(ms + EPS) * nw_ref[...]).astype(o_ref.dtype)


def _group_lanes(a, n_groups):
    rows, heads = a.shape
    hpg = heads // n_groups
    a = a.reshape(rows, n_groups, hpg)
    a = jnp.pad(a, ((0, 0), (0, 0), (0, V7X_LANES - hpg)))
    return a.reshape(rows, n_groups * V7X_LANES)


def _ssd(zxbc, dt_raw, conv_w, conv_b, dt_bias, a_log, d_skip, norm_w, *, batch, seq):
    T = zxbc.shape[0]
    G = SSM_GROUPS
    N = SSM_STATE
    heads = dt_bias.shape[0]
    inner = heads * SSM_HEADDIM
    W = inner // G
    L = SSD_CHUNK
    nc = seq // L
    xb, bb, cb_ = inner // W, (2 * inner) // N, (2 * inner + G * N) // N
    wxb, wbb, wcb = 0, inner // N, (inner + G * N) // N

    dt_g = _group_lanes(dt_raw[:, :heads], G)
    dtb_g = _group_lanes(dt_bias.reshape(1, heads), G)
    alog_g = _group_lanes(a_log.reshape(1, heads), G)
    dsk_g = _group_lanes(d_skip.reshape(1, heads), G)
    conv_b2 = conv_b.reshape(1, -1)

    rows = lambda b, c, g: b * nc + c
    return pl.pallas_call(
        _ssd_kernel,
        grid=(batch, nc, G),
        in_specs=[
            pl.BlockSpec((L, W), lambda b, c, g: (rows(b, c, g), g)),
            pl.BlockSpec((L, W), lambda b, c, g: (rows(b, c, g), xb + g)),
            pl.BlockSpec((L, N), lambda b, c, g: (rows(b, c, g), bb + g)),
            pl.BlockSpec((L, N), lambda b, c, g: (rows(b, c, g), cb_ + g)),
            pl.BlockSpec((L, V7X_LANES), lambda b, c, g: (rows(b, c, g), g)),
            pl.BlockSpec((SSM_CONV, W), lambda b, c, g: (0, wxb + g)),
            pl.BlockSpec((SSM_CONV, N), lambda b, c, g: (0, wbb + g)),
            pl.BlockSpec((SSM_CONV, N), lambda b, c, g: (0, wcb + g)),
            pl.BlockSpec((1, W), lambda b, c, g: (0, wxb + g)),
            pl.BlockSpec((1, N), lambda b, c, g: (0, wbb + g)),
            pl.BlockSpec((1, N), lambda b, c, g: (0, wcb + g)),
            pl.BlockSpec((1, V7X_LANES), lambda b, c, g: (0, g)),
            pl.BlockSpec((1, V7X_LANES), lambda b, c, g: (0, g)),
            pl.BlockSpec((1, V7X_LANES), lambda b, c, g: (0, g)),
            pl.BlockSpec((1, W), lambda b, c, g: (0, g)),
        ],
        out_specs=pl.BlockSpec((L, W), lambda b, c, g: (rows(b, c, g), g)),
        out_shape=jax.ShapeDtypeStruct((T, inner), BF16),
        scratch_shapes=[
            pltpu.VMEM((G, N, W), F32),
            pltpu.VMEM((G, V7X_SUBLANES, W + 2 * N), F32),
        ],
        compiler_params=_cparams(("parallel", "arbitrary", "arbitrary"), 48 * 1024 * 1024),
        name="conv_ssd_scan",
    )(zxbc, zxbc, zxbc, zxbc, dt_g, conv_w, conv_w, conv_w, conv_b2, conv_b2, conv_b2,
      dtb_g, alog_g, dsk_g, norm_w.reshape(1, inner))


def kernel(x, ln_mix, ln_ffn, w_in_even, w_out_even, hgrn_lb, hgrn_norm, q_norm, k_norm, w_in_ssm, conv_w,
           conv_b, dt_bias, a_log, d_skip, ssm_norm, w_out_ssm, w_gate, w_up, w_down):
    batch, seq, d_model = x.shape
    depth = ln_mix.shape[0]
    T = batch * seq
    xf = x.reshape(T, d_model)

    hg_width = hgrn_lb.shape[1]
    hg_heads = hg_width // HG_DIM
    mb_width = (w_in_even.shape[2] - 4 * hg_width) // 3
    mb_heads = mb_width // MB_DIM
    ssm_heads = dt_bias.shape[1]
    ssm_main = w_in_ssm.shape[2] - ssm_heads

    tm = min(1024, T)
    tm_ffn = min(512, T)
    for layer in range(depth):
        if layer % 2 == 0:
            e = layer // 2
            proj = _norm_matmul(xf, ln_mix[layer], w_in_even[e].astype(BF16), tm=tm, tn=1024)
            o_a = _hgrn(proj, hgrn_lb, hgrn_norm[e], batch=batch, seq=seq, n_heads=hg_heads, col0=0, slot=e)
            o_b = _moba(proj, q_norm[e], k_norm[e], batch=batch, seq=seq, n_heads=mb_heads, col0=4 * hg_width)
            xf = _proj_residual([o_a, o_b], w_out_even[e].astype(BF16), xf, tm=tm, tn=512)
        else:
            o = layer // 2
            w_in = w_in_ssm[o]
            w_dt = jnp.pad(w_in[:, ssm_main:], ((0, 0), (0, V7X_LANES - ssm_heads))).astype(BF16)
            zxbc, dt_raw = _norm_matmul(xf, ln_mix[layer], w_in[:, :ssm_main].astype(BF16), w_dt, tm=tm, tn=1024)
            gy = _ssd(zxbc, dt_raw, conv_w[o], conv_b[o], dt_bias[o], a_log[o], d_skip[o], ssm_norm[o],
                      batch=batch, seq=seq)
            xf = _proj_residual([gy], w_out_ssm[o].astype(BF16), xf, tm=tm, tn=512)
        xf = _ffn(xf, ln_ffn[layer], w_gate[layer].astype(BF16), w_up[layer].astype(BF16),
                  w_down[layer].astype(BF16), tm=tm_ffn, tf=512)
    return xf.reshape(batch, seq, d_model)
```

```python
import functools
import math

import jax
import jax.numpy as jnp
from jax import lax
from jax.experimental import pallas as pl
from jax.experimental.pallas import tpu as pltpu

F32 = jnp.float32
BF16 = jnp.bfloat16
EPS = 1e-6
NEG_INF = float("-inf")

HG_DIM = 128
MB_DIM = 128
MB_BLOCK = 256
MB_TOPK = 3
SSM_HEADDIM = 64
SSM_STATE = 128
SSM_GROUPS = 8
SSM_CONV = 4

V7X_LANES = 128
V7X_SUBLANES = 8
V7X_VMEM_BYTES = 64 * 1024 * 1024

HGRN_CHUNK = 256
SSD_CHUNK = 256
NORM_ROWS = 128
FFN_OUT_COLS = 512


def _cparams(semantics, vmem_bytes):
    return pltpu.CompilerParams(dimension_semantics=semantics, vmem_limit_bytes=int(vmem_bytes))


def _vmem_limit(estimate_bytes):
    return min(V7X_VMEM_BYTES - 6 * 1024 * 1024, max(32 * 1024 * 1024, int(estimate_bytes * 1.25)))


def _silu(x):
    return x * jax.nn.sigmoid(x)


def _rmsnorm_rows(x_ref, gain_ref, h_ref):
    n_steps = x_ref.shape[0] // NORM_ROWS

    def body(r, carry):
        rows = pl.ds(pl.multiple_of(r * NORM_ROWS, NORM_ROWS), NORM_ROWS)
        x = x_ref[rows, :]
        ms = jnp.mean(x * x, axis=-1, keepdims=True)
        h_ref[rows, :] = (x * lax.rsqrt(ms + EPS) * gain_ref[...]).astype(h_ref.dtype)
        return carry

    lax.fori_loop(0, n_steps, body, 0)


def _norm_matmul_kernel(x_ref, gain_ref, w_ref, o_ref, h_ref):
    @pl.when(pl.program_id(1) == 0)
    def _():
        _rmsnorm_rows(x_ref, gain_ref, h_ref)

    o_ref[...] = jnp.dot(h_ref[...], w_ref[...], preferred_element_type=F32).astype(o_ref.dtype)


def _norm_matmul_extra_kernel(x_ref, gain_ref, w_ref, w2_ref, o_ref, o2_ref, h_ref):
    @pl.when(pl.program_id(1) == 0)
    def _():
        _rmsnorm_rows(x_ref, gain_ref, h_ref)
        o2_ref[...] = jnp.dot(h_ref[...], w2_ref[...], preferred_element_type=F32)

    o_ref[...] = jnp.dot(h_ref[...], w_ref[...], preferred_element_type=F32).astype(o_ref.dtype)


def _norm_matmul(x, gain, w, w_extra=None, *, layer, n_cols, tm, tn):
    T, D = x.shape
    N = n_cols
    grid = (T // tm, N // tn)
    in_specs = [
        pl.BlockSpec((tm, D), lambda i, j: (i, 0)),
        pl.BlockSpec((1, D), lambda i, j: (0, 0)),
        pl.BlockSpec((None, D, tn), lambda i, j: (layer, 0, j)),
    ]
    out_shape = [jax.ShapeDtypeStruct((T, N), F32)]
    out_specs = [pl.BlockSpec((tm, tn), lambda i, j: (i, j))]
    args = [x, gain.reshape(1, D), w]
    est = 2 * tm * D * 4 + tm * D * 2 + 2 * D * tn * 2 + 3 * tm * tn * 4
    if w_extra is None:
        body = _norm_matmul_kernel
    else:
        body = _norm_matmul_extra_kernel
        n2 = w_extra.shape[1]
        in_specs.append(pl.BlockSpec((D, n2), lambda i, j: (0, 0)))
        out_shape.append(jax.ShapeDtypeStruct((T, n2), F32))
        out_specs.append(pl.BlockSpec((tm, n2), lambda i, j: (i, 0)))
        args.append(w_extra)
        est += 2 * D * n2 * 2 + 2 * tm * n2 * 4
    res = pl.pallas_call(
        body,
        grid=grid,
        in_specs=in_specs,
        out_specs=out_specs,
        out_shape=out_shape,
        scratch_shapes=[pltpu.VMEM((tm, D), BF16)],
        compiler_params=_cparams(("parallel", "arbitrary"), _vmem_limit(est)),
        name="norm_in_proj",
    )(*args)
    return res[0] if w_extra is None else res


def _proj_residual_kernel(*refs, n_lhs):
    lhs_refs = refs[:n_lhs]
    w_ref, x_ref, o_ref = refs[n_lhs:]
    acc = x_ref[...]
    k0 = 0
    for a_ref in lhs_refs:
        k1 = k0 + a_ref.shape[1]
        acc = acc + jnp.dot(a_ref[...], w_ref[k0:k1, :], preferred_element_type=F32)
        k0 = k1
    o_ref[...] = acc


def _proj_residual(lhs_list, w, x, *, layer, tm, tn):
    T, N = x.shape
    K = w.shape[1]
    grid = (T // tm, N // tn)
    in_specs = [pl.BlockSpec((tm, a.shape[1]), lambda i, j: (i, 0)) for a in lhs_list]
    in_specs += [
        pl.BlockSpec((None, K, tn), lambda i, j: (layer, 0, j)),
        pl.BlockSpec((tm, tn), lambda i, j: (i, j)),
    ]
    est = 2 * tm * K * 2 + 2 * K * tn * 2 + 5 * tm * tn * 4
    return pl.pallas_call(
        functools.partial(_proj_residual_kernel, n_lhs=len(lhs_list)),
        grid=grid,
        in_specs=in_specs,
        out_specs=pl.BlockSpec((tm, tn), lambda i, j: (i, j)),
        out_shape=jax.ShapeDtypeStruct((T, N), F32),
        compiler_params=_cparams(("parallel", "arbitrary"), _vmem_limit(est)),
        name="out_proj_residual",
    )(*lhs_list, w, x)


def _ffn_kernel(x_ref, gain_ref, wg_ref, wu_ref, wd_ref, o_ref, h_ref):
    f = pl.program_id(1)

    @pl.when(f == 0)
    def _():
        _rmsnorm_rows(x_ref, gain_ref, h_ref)
        o_ref[...] = x_ref[...]

    h = h_ref[...]
    g = jnp.dot(h, wg_ref[...], preferred_element_type=F32)
    u = jnp.dot(h, wu_ref[...], preferred_element_type=F32)
    a = (_silu(g) * u).astype(BF16)
    n_out = o_ref.shape[1]
    for c0 in range(0, n_out, FFN_OUT_COLS):
        cols = slice(c0, min(c0 + FFN_OUT_COLS, n_out))
        o_ref[:, cols] += jnp.dot(a, wd_ref[:, cols], preferred_element_type=F32)


def _ffn(x, gain, wg, wu, wd, *, layer, tm, tf):
    T, D = x.shape
    F = wg.shape[2]
    grid = (T // tm, F // tf)
    est = 3 * tm * D * 4 + tm * D * 2 + 3 * 2 * D * tf * 2 + 3 * tm * tf * 4 + tm * FFN_OUT_COLS * 4
    return pl.pallas_call(
        _ffn_kernel,
        grid=grid,
        in_specs=[
            pl.BlockSpec((tm, D), lambda i, f: (i, 0), pipeline_mode=pl.Buffered(1)),
            pl.BlockSpec((1, D), lambda i, f: (0, 0)),
            pl.BlockSpec((None, D, tf), lambda i, f: (layer, 0, f)),
            pl.BlockSpec((None, D, tf), lambda i, f: (layer, 0, f)),
            pl.BlockSpec((None, tf, D), lambda i, f: (layer, f, 0)),
        ],
        out_specs=pl.BlockSpec((tm, D), lambda i, f: (i, 0)),
        out_shape=jax.ShapeDtypeStruct((T, D), F32),
        scratch_shapes=[pltpu.VMEM((tm, D), BF16)],
        compiler_params=_cparams(("parallel", "arbitrary"), _vmem_limit(est)),
        name="swiglu_ffn",
    )(x, gain.reshape(1, D), wg, wu, wd)


def _cumsum_rows(a, row):
    n = a.shape[0]
    shift = 1
    while shift < n:
        a = a + jnp.where(row >= shift, pltpu.roll(a, shift, 0), 0.0)
        shift *= 2
    return a


def _hgrn_kernel(q_ref, f_ref, i_ref, g_ref, lb_ref, nw_ref, o_ref, st_ref, *, slot):
    C, D = q_ref.shape

    @pl.when(pl.program_id(2) == 0)
    def _():
        st_ref[...] = jnp.zeros_like(st_ref)

    lbl = lb_ref[...]
    ex = jnp.exp(lbl - jnp.max(lbl, axis=0, keepdims=True))
    lb = jnp.sum(ex[: slot + 1], axis=0, keepdims=True) / jnp.sum(ex, axis=0, keepdims=True)

    fr = f_ref[...]
    q = _silu(q_ref[...])
    log_f = jnp.log(lb + (1.0 - lb) * jax.nn.sigmoid(fr))
    k = (1.0 - lb) * jax.nn.sigmoid(-fr)
    v = i_ref[...]

    row = lax.broadcasted_iota(jnp.int32, (C, D), 0)
    b = _cumsum_rows(log_f, row)

    tt = lax.broadcasted_iota(jnp.int32, (C, C), 0)
    ss = lax.broadcasted_iota(jnp.int32, (C, C), 1)
    differ = tt ^ ss
    attn = jnp.where(differ == 0, jnp.sum(q * k, axis=1, keepdims=True), 0.0)

    rolled = {}

    def shifted(o):
        if o not in rolled:
            rolled[o] = pltpu.roll(b, o % C, 0)
        return rolled[o]

    n_levels = int(math.log2(C))
    for lvl in range(n_levels):
        half = 1 << lvl
        blk = 2 * half
        if blk <= V7X_SUBLANES:
            off = (row & (blk - 1)) - (half - 1)
            beta = b
            for o in range(-(half - 1), half + 1):
                if o != 0:
                    beta = jnp.where(off == o, shifted(o), beta)
        else:
            mid = b.reshape(C // blk, blk, D)[:, half - 1:half, :]
            beta = jnp.broadcast_to(mid, (C // blk, blk, D)).reshape(C, D)
        is_q = (row & half) != 0
        e = jnp.exp(jnp.where(is_q, b - beta, beta - b))
        qt = jnp.where(is_q, q * e, 0.0).astype(BF16)
        kt = jnp.where(is_q, 0.0, k * e).astype(BF16)
        p = lax.dot_general(qt, kt, (((1,), (1,)), ((), ())), preferred_element_type=F32)
        if blk < C:
            p = jnp.where(differ < blk, p, 0.0)
        attn = attn + p

    st = st_ref[...]
    o = jnp.dot(attn.astype(BF16), v.astype(BF16), preferred_element_type=F32)
    qe = (q * jnp.exp(b)).astype(BF16)
    o = o + lax.dot_general(qe, st.astype(BF16), (((1,), (1,)), ((), ())), preferred_element_type=F32)

    b_last = b[C - 1:C, :]
    k_dec = (k * jnp.exp(b_last - b)).astype(BF16)
    v_t = v.T.astype(BF16)
    st_ref[...] = st * jnp.exp(b_last) + jnp.dot(v_t, k_dec, preferred_element_type=F32)

    ms = jnp.mean(o * o, axis=-1, keepdims=True)
    o = o * lax.rsqrt(ms + EPS) * nw_ref[...] * _silu(g_ref[...])
    o_ref[...] = o.astype(o_ref.dtype)


def _hgrn(proj, lb_table, norm_w, *, batch, seq, n_heads, col0, slot):
    T = proj.shape[0]
    C = HGRN_CHUNK
    nc = seq // C
    cb0 = col0 // HG_DIM
    n_slots = lb_table.shape[0]

    def sec(s):
        return pl.BlockSpec((C, HG_DIM), lambda b, h, c: (b * nc + c, cb0 + s * n_heads + h))

    return pl.pallas_call(
        functools.partial(_hgrn_kernel, slot=slot),
        grid=(batch, n_heads, nc),
        in_specs=[
            sec(0), sec(1), sec(2), sec(3),
            pl.BlockSpec((n_slots, HG_DIM), lambda b, h, c: (0, h)),
            pl.BlockSpec((1, HG_DIM), lambda b, h, c: (0, 0)),
        ],
        out_specs=pl.BlockSpec((C, HG_DIM), lambda b, h, c: (b * nc + c, h)),
        out_shape=jax.ShapeDtypeStruct((T, n_heads * HG_DIM), BF16),
        scratch_shapes=[pltpu.VMEM((HG_DIM, HG_DIM), F32)],
        compiler_params=_cparams(("parallel", "parallel", "arbitrary"), 32 * 1024 * 1024),
        name="hgrn2_scan",
    )(proj, proj, proj, proj, lb_table, norm_w.reshape(1, HG_DIM))


def _moba_kernel(q_ref, k_ref, v_ref, qw_ref, kw_ref, o_ref):
    S, D = q_ref.shape
    BLK = MB_BLOCK
    nb = S // BLK
    scale = D ** -0.5

    q = q_ref[...]
    k = k_ref[...]
    qn = q * lax.rsqrt(jnp.mean(q * q, axis=-1, keepdims=True) + EPS) * qw_ref[...]
    kn = k * lax.rsqrt(jnp.mean(k * k, axis=-1, keepdims=True) + EPS) * kw_ref[...]
    k_mean = jnp.mean(kn.reshape(nb, BLK, D), axis=1)
    qn_t = qn.T
    gate_t = jnp.dot(k_mean, qn_t, precision=lax.Precision.HIGHEST,
                     preferred_element_type=F32)
    qs_t = (qn_t * scale).astype(BF16)
    kn_b = kn.astype(BF16)
    v_t = v_ref[...].T.astype(BF16)

    kk = lax.broadcasted_iota(jnp.int32, (BLK, BLK), 0)
    qq = lax.broadcasted_iota(jnp.int32, (BLK, BLK), 1)
    causal = kk <= qq

    for j in range(nb):
        cols = slice(j * BLK, (j + 1) * BLK)
        q_j = qs_t[:, cols]
        n_keep = min(MB_TOPK, j)
        selected = None
        if j > n_keep:
            g = [gate_t[m:m + 1, cols] for m in range(j)]
            selected = []
            for n in range(j):
                beaten = jnp.zeros((1, BLK), jnp.int32)
                for m in range(j):
                    if m < n:
                        beaten = beaten + (g[m] >= g[n]).astype(jnp.int32)
                    elif m > n:
                        beaten = beaten + (g[m] > g[n]).astype(jnp.int32)
                selected.append(beaten < n_keep)
        scores = []
        for n in range(j + 1):
            s = jnp.dot(kn_b[n * BLK:(n + 1) * BLK, :], q_j, preferred_element_type=F32)
            if n == j:
                s = jnp.where(causal, s, NEG_INF)
            elif selected is not None:
                s = jnp.where(selected[n], s, NEG_INF)
            scores.append(s)
        m_run = jnp.max(scores[0], axis=0, keepdims=True)
        for s in scores[1:]:
            m_run = jnp.maximum(m_run, jnp.max(s, axis=0, keepdims=True))
        denom = jnp.zeros((1, BLK), F32)
        acc = jnp.zeros((D, BLK), F32)
        for n, s in enumerate(scores):
            p = jnp.exp(s - m_run)
            denom = denom + jnp.sum(p, axis=0, keepdims=True)
            acc = acc + jnp.dot(v_t[:, n * BLK:(n + 1) * BLK], p.astype(BF16), preferred_element_type=F32)
        o_ref[j * BLK:(j + 1) * BLK, :] = (acc / denom).T.astype(o_ref.dtype)


def _moba(proj, q_norm_w, k_norm_w, *, batch, seq, n_heads, col0):
    T = proj.shape[0]
    cb0 = col0 // MB_DIM

    def sec(s):
        return pl.BlockSpec((seq, MB_DIM), lambda b, h: (b, cb0 + s * n_heads + h))

    return pl.pallas_call(
        _moba_kernel,
        grid=(batch, n_heads),
        in_specs=[
            sec(0), sec(1), sec(2),
            pl.BlockSpec((1, MB_DIM), lambda b, h: (0, 0)),
            pl.BlockSpec((1, MB_DIM), lambda b, h: (0, 0)),
        ],
        out_specs=pl.BlockSpec((seq, MB_DIM), lambda b, h: (b, h)),
        out_shape=jax.ShapeDtypeStruct((T, n_heads * MB_DIM), BF16),
        compiler_params=_cparams(("parallel", "parallel"), 48 * 1024 * 1024),
        name="moba_attention",
    )(proj, proj, proj, q_norm_w.reshape(1, MB_DIM), k_norm_w.reshape(1, MB_DIM))


def _causal_conv_silu(raw, tail, w_ref, bias_ref):
    K = SSM_CONV
    acc = raw * w_ref[K - 1:K, :] + bias_ref[...]
    row8 = lax.broadcasted_iota(jnp.int32, tail.shape, 0)
    for d in range(1, K):
        sh = pltpu.roll(raw, d, 0)
        head = jnp.where(row8 < d, pltpu.roll(tail, d, 0), sh[:V7X_SUBLANES])
        sh = jnp.concatenate([head, sh[V7X_SUBLANES:]], axis=0)
        acc = acc + sh * w_ref[K - 1 - d:K - d, :]
    return _silu(acc)


def _expand_heads(cols, n_heads, width):
    rows = cols.shape[0]
    per_vreg = V7X_LANES // width
    lane = lax.broadcasted_iota(jnp.int32, (rows, V7X_LANES), 1)
    blocks = []
    for h0 in range(0, n_heads, per_vreg):
        blk = jnp.broadcast_to(cols[:, h0:h0 + 1], (rows, V7X_LANES))
        for k in range(1, per_vreg):
            nxt = jnp.broadcast_to(cols[:, h0 + k:h0 + k + 1], (rows, V7X_LANES))
            blk = jnp.where(lane >= k * width, nxt, blk)
        blocks.append(blk)
    return jnp.concatenate(blocks, axis=1)


def _ssd_kernel(z_ref, x_ref, bm_ref, cm_ref, dt_ref, cwx_ref, cwb_ref, cwc_ref, cbx_ref, cbb_ref, cbc_ref,
                dtb_ref, alog_ref, dsk_ref, nw_ref, o_ref, st_ref, tail_ref, cs_ref, ecs_ref, wd_ref, cst_ref,
                dtt_ref):
    L, W = x_ref.shape
    P = SSM_HEADDIM
    hpg = W // P
    N = bm_ref.shape[1]
    g = pl.program_id(2)

    @pl.when(pl.program_id(1) == 0)
    def _():
        st_ref[g] = jnp.zeros(st_ref.shape[1:], F32)
        tail_ref[g] = jnp.zeros(tail_ref.shape[1:], F32)

    @pl.when(g == 0)
    def _():
        pre = dt_ref[...] + dtb_ref[...]
        dt_all = jnp.maximum(pre, 0.0) + jnp.log1p(jnp.exp(-jnp.abs(pre)))
        a_all = dt_all * (-jnp.exp(alog_ref[...]))
        row = lax.broadcasted_iota(jnp.int32, a_all.shape, 0)
        cs_all = _cumsum_rows(a_all, row)
        cs_ref[...] = cs_all
        ecs_ref[...] = jnp.exp(cs_all)
        wd_ref[...] = jnp.exp(cs_all[L - 1:L, :] - cs_all) * dt_all
        cst_ref[...] = cs_all.T
        dtt_ref[...] = dt_all.T

    x_raw = x_ref[...]
    b_raw = bm_ref[...]
    c_raw = cm_ref[...]
    tail = tail_ref[g]
    xs = _causal_conv_silu(x_raw, tail[:, :W], cwx_ref, cbx_ref)
    bm = _causal_conv_silu(b_raw, tail[:, W:W + N], cwb_ref, cbb_ref)
    cm = _causal_conv_silu(c_raw, tail[:, W + N:], cwc_ref, cbc_ref)
    tail_ref[g] = jnp.concatenate(
        [x_raw[L - V7X_SUBLANES:], b_raw[L - V7X_SUBLANES:], c_raw[L - V7X_SUBLANES:]], axis=1)

    shift = (V7X_LANES - g * hpg) % V7X_LANES
    cs = pltpu.roll(cs_ref[...], shift, 1)
    ecs = pltpu.roll(ecs_ref[...], shift, 1)
    wd = pltpu.roll(wd_ref[...], shift, 1)
    dsk = pltpu.roll(dsk_ref[...], shift, 1)
    head_rows = pl.ds(pl.multiple_of(g * hpg, hpg), hpg)
    cs_t = cst_ref[head_rows, :]
    dt_t = dtt_ref[head_rows, :]

    xs_b = xs.astype(BF16)
    bm_b = bm.astype(BF16)
    cm_b = cm.astype(BF16)
    cb = lax.dot_general(cm_b, bm_b, (((1,), (1,)), ((), ())), preferred_element_type=F32)
    tt = lax.broadcasted_iota(jnp.int32, (L, L), 0)
    ss = lax.broadcasted_iota(jnp.int32, (L, L), 1)
    causal = tt >= ss

    def head_matrix(r):
        seg = cs[:, r:r + 1] - cs_t[r:r + 1, :]
        lmat = jnp.exp(jnp.where(causal, seg, NEG_INF))
        return (cb * lmat * dt_t[r:r + 1, :]).astype(BF16)

    per_vreg = V7X_LANES // P
    lane_head = lax.broadcasted_iota(jnp.int32, (L, V7X_LANES), 1) // P
    zero = jnp.zeros((L, V7X_LANES), BF16)
    y_blocks = []
    for h0 in range(0, hpg, per_vreg):
        x_blk = xs_b[:, h0 * P:h0 * P + V7X_LANES]
        lhs = jnp.concatenate([head_matrix(h0 + k) for k in range(per_vreg)], axis=1)
        rhs = jnp.concatenate([jnp.where(lane_head == k, x_blk, zero) for k in range(per_vreg)], axis=0)
        y_blocks.append(jnp.dot(lhs, rhs, preferred_element_type=F32))
    y = jnp.concatenate(y_blocks, axis=1)

    st = st_ref[g]
    y = y + jnp.dot(cm_b, st.astype(BF16), preferred_element_type=F32) * _expand_heads(ecs, hpg, P)
    y = y + _expand_heads(dsk, hpg, P) * xs

    xw = (xs * _expand_heads(wd, hpg, P)).astype(BF16)
    bm_t = bm.T.astype(BF16)
    st_ref[g] = st * _expand_heads(ecs[L - 1:L, :], hpg, P) + jnp.dot(bm_t, xw, preferred_element_type=F32)

    gy = y * _silu(z_ref[...])
    ms = jnp.mean(gy * gy, axis=-1, keepdims=True)
    o_ref[...] = (gy * lax.rsqrt(ms + EPS) * nw_ref[...]).astype(o_ref.dtype)


def _pad_lanes(a):
    return jnp.pad(a.reshape(1, -1), ((0, 0), (0, V7X_LANES - a.shape[0])))


def _ssd(zxbc, dt_raw, conv_w, conv_b, dt_bias, a_log, d_skip, norm_w, *, layer, batch, seq):
    T = zxbc.shape[0]
    G = SSM_GROUPS
    N = SSM_STATE
    heads = dt_bias.shape[1]
    assert heads <= V7X_LANES and dt_raw.shape[1] == V7X_LANES
    inner = heads * SSM_HEADDIM
    W = inner // G
    L = SSD_CHUNK
    nc = seq // L
    xb, bb, cb_ = inner // W, (2 * inner) // N, (2 * inner + G * N) // N
    wxb, wbb, wcb = 0, inner // N, (inner + G * N) // N

    dtb_p = _pad_lanes(dt_bias[layer])
    alog_p = _pad_lanes(a_log[layer])
    dsk_p = _pad_lanes(d_skip[layer])
    conv_b3 = conv_b.reshape(conv_b.shape[0], 1, -1)
    norm_w3 = norm_w.reshape(norm_w.shape[0], 1, -1)

    rows = lambda b, c, g: b * nc + c
    return pl.pallas_call(
        _ssd_kernel,
        grid=(batch, nc, G),
        in_specs=[
            pl.BlockSpec((L, W), lambda b, c, g: (rows(b, c, g), g)),
            pl.BlockSpec((L, W), lambda b, c, g: (rows(b, c, g), xb + g)),
            pl.BlockSpec((L, N), lambda b, c, g: (rows(b, c, g), bb + g)),
            pl.BlockSpec((L, N), lambda b, c, g: (rows(b, c, g), cb_ + g)),
            pl.BlockSpec((L, V7X_LANES), lambda b, c, g: (rows(b, c, g), 0)),
            pl.BlockSpec((None, SSM_CONV, W), lambda b, c, g: (layer, 0, wxb + g)),
            pl.BlockSpec((None, SSM_CONV, N), lambda b, c, g: (layer, 0, wbb + g)),
            pl.BlockSpec((None, SSM_CONV, N), lambda b, c, g: (layer, 0, wcb + g)),
            pl.BlockSpec((None, 1, W), lambda b, c, g: (layer, 0, wxb + g)),
            pl.BlockSpec((None, 1, N), lambda b, c, g: (layer, 0, wbb + g)),
            pl.BlockSpec((None, 1, N), lambda b, c, g: (layer, 0, wcb + g)),
            pl.BlockSpec((1, V7X_LANES), lambda b, c, g: (0, 0)),
            pl.BlockSpec((1, V7X_LANES), lambda b, c, g: (0, 0)),
            pl.BlockSpec((1, V7X_LANES), lambda b, c, g: (0, 0)),
            pl.BlockSpec((None, 1, W), lambda b, c, g: (layer, 0, g)),
        ],
        out_specs=pl.BlockSpec((L, W), lambda b, c, g: (rows(b, c, g), g)),
        out_shape=jax.ShapeDtypeStruct((T, inner), BF16),
        scratch_shapes=[
            pltpu.VMEM((G, N, W), F32),
            pltpu.VMEM((G, V7X_SUBLANES, W + 2 * N), F32),
            pltpu.VMEM((L, V7X_LANES), F32),
            pltpu.VMEM((L, V7X_LANES), F32),
            pltpu.VMEM((L, V7X_LANES), F32),
            pltpu.VMEM((V7X_LANES, L), F32),
            pltpu.VMEM((V7X_LANES, L), F32),
        ],
        compiler_params=_cparams(("parallel", "arbitrary", "arbitrary"), 48 * 1024 * 1024),
        name="conv_ssd_scan",
    )(zxbc, zxbc, zxbc, zxbc, dt_raw, conv_w, conv_w, conv_w, conv_b3, conv_b3, conv_b3,
      dtb_p, alog_p, dsk_p, norm_w3)


def kernel(x, ln_mix, ln_ffn, w_in_even, w_out_even, hgrn_lb, hgrn_norm, q_norm, k_norm, w_in_ssm, conv_w,
           conv_b, dt_bias, a_log, d_skip, ssm_norm, w_out_ssm, w_gate, w_up, w_down):
    batch, seq, d_model = x.shape
    depth = ln_mix.shape[0]
    T = batch * seq
    xf = x.reshape(T, d_model)

    hg_width = hgrn_lb.shape[1]
    hg_heads = hg_width // HG_DIM
    mb_width = (w_in_even.shape[2] - 4 * hg_width) // 3
    mb_heads = mb_width // MB_DIM
    ssm_heads = dt_bias.shape[1]
    ssm_main = w_in_ssm.shape[2] - ssm_heads

    w_in_even_b = w_in_even.astype(BF16)
    w_out_even_b = w_out_even.astype(BF16)
    w_in_ssm_b = w_in_ssm.astype(BF16)
    w_out_ssm_b = w_out_ssm.astype(BF16)
    w_gate_b = w_gate.astype(BF16)
    w_up_b = w_up.astype(BF16)
    w_down_b = w_down.astype(BF16)

    tm = min(1024, T)
    for layer in range(depth):
        if layer % 2 == 0:
            e = layer // 2
            proj = _norm_matmul(xf, ln_mix[layer], w_in_even_b, layer=e, n_cols=w_in_even.shape[2], tm=tm, tn=1024)
            o_a = _hgrn(proj, hgrn_lb, hgrn_norm[e], batch=batch, seq=seq, n_heads=hg_heads, col0=0, slot=e)
            o_b = _moba(proj, q_norm[e], k_norm[e], batch=batch, seq=seq, n_heads=mb_heads, col0=4 * hg_width)
            xf = _proj_residual([o_a, o_b], w_out_even_b, xf, layer=e, tm=tm, tn=512)
        else:
            o = layer // 2
            w_dt = jnp.pad(w_in_ssm[o][:, ssm_main:], ((0, 0), (0, V7X_LANES - ssm_heads))).astype(BF16)
            zxbc, dt_raw = _norm_matmul(xf, ln_mix[layer], w_in_ssm_b, w_dt, layer=o, n_cols=ssm_main, tm=tm, tn=1024)
            gy = _ssd(zxbc, dt_raw, conv_w, conv_b, dt_bias, a_log, d_skip, ssm_norm, layer=o, batch=batch, seq=seq)
            xf = _proj_residual([gy], w_out_ssm_b, xf, layer=o, tm=tm, tn=512)
        xf = _ffn(xf, ln_ffn[layer], w_gate_b, w_up_b, w_down_b, layer=layer, tm=tm, tf=512)
    return xf.reshape(batch, seq, d_model)
```

```python
import functools
import math

import jax
import jax.numpy as jnp
import numpy as np
from jax import lax
from jax.experimental import pallas as pl
from jax.experimental.pallas import tpu as pltpu

F32 = jnp.float32
BF16 = jnp.bfloat16
EPS = 1e-6
NEG_INF = float("-inf")
LOG2_E = 1.4426950408889634

HG_DIM = 128
MB_DIM = 128
MB_BLOCK = 256
MB_TOPK = 3
SSM_HEADDIM = 64
SSM_STATE = 128
SSM_GROUPS = 8
SSM_CONV = 4

V7X_LANES = 128
V7X_SUBLANES = 8
V7X_VMEM_BYTES = 64 * 1024 * 1024

HGRN_CHUNK = 256
HGRN_HEADS_PER_STEP = 8
SSD_GROUPS_PER_STEP = 8
MOBA_HEADS_PER_STEP = 2
SSD_CHUNK = 256
NORM_ROWS = 128
FFN_OUT_COLS = 512


def _cparams(semantics, vmem_bytes):
    return pltpu.CompilerParams(dimension_semantics=semantics, vmem_limit_bytes=int(vmem_bytes))


def _vmem_limit(estimate_bytes):
    return min(V7X_VMEM_BYTES - 6 * 1024 * 1024, max(32 * 1024 * 1024, int(estimate_bytes * 1.25)))


def _silu(x):
    h = 0.5 * x
    return h + h * jnp.tanh(h)


def _rmsnorm_rows(x_ref, gain_ref, h_ref):
    n_steps = x_ref.shape[0] // NORM_ROWS

    def body(r, carry):
        rows = pl.ds(pl.multiple_of(r * NORM_ROWS, NORM_ROWS), NORM_ROWS)
        x = x_ref[rows, :]
        ms = jnp.mean(x * x, axis=-1, keepdims=True)
        h_ref[rows, :] = (x * lax.rsqrt(ms + EPS) * gain_ref[...]).astype(h_ref.dtype)
        return carry

    lax.fori_loop(0, n_steps, body, 0)


def _norm_matmul_kernel(x_ref, gain_ref, w_ref, o_ref, h_ref):
    @pl.when(pl.program_id(1) == 0)
    def _():
        _rmsnorm_rows(x_ref, gain_ref, h_ref)

    o_ref[...] = jnp.dot(h_ref[...], w_ref[...], preferred_element_type=F32).astype(o_ref.dtype)


def _norm_matmul_extra_kernel(x_ref, gain_ref, w_ref, w2_ref, o_ref, o2_ref, h_ref):
    @pl.when(pl.program_id(1) == 0)
    def _():
        _rmsnorm_rows(x_ref, gain_ref, h_ref)
        o2_ref[...] = jnp.dot(h_ref[...], w2_ref[...], preferred_element_type=F32)

    o_ref[...] = jnp.dot(h_ref[...], w_ref[...], preferred_element_type=F32).astype(o_ref.dtype)


def _norm_matmul(x, gain, w, w_extra=None, *, layer, n_cols, tm, tn):
    T, D = x.shape
    N = n_cols
    grid = (T // tm, N // tn)
    in_specs = [
        pl.BlockSpec((tm, D), lambda i, j: (i, 0)),
        pl.BlockSpec((1, D), lambda i, j: (0, 0)),
        pl.BlockSpec((None, D, tn), lambda i, j: (layer, 0, j)),
    ]
    out_shape = [jax.ShapeDtypeStruct((T, N), F32)]
    out_specs = [pl.BlockSpec((tm, tn), lambda i, j: (i, j))]
    args = [x, gain.reshape(1, D), w]
    est = 2 * tm * D * 4 + tm * D * 2 + 2 * D * tn * 2 + 3 * tm * tn * 4
    if w_extra is None:
        body = _norm_matmul_kernel
    else:
        body = _norm_matmul_extra_kernel
        n2 = w_extra.shape[1]
        in_specs.append(pl.BlockSpec((D, n2), lambda i, j: (0, 0)))
        out_shape.append(jax.ShapeDtypeStruct((T, n2), F32))
        out_specs.append(pl.BlockSpec((tm, n2), lambda i, j: (i, 0)))
        args.append(w_extra)
        est += 2 * D * n2 * 2 + 2 * tm * n2 * 4
    res = pl.pallas_call(
        body,
        grid=grid,
        in_specs=in_specs,
        out_specs=out_specs,
        out_shape=out_shape,
        scratch_shapes=[pltpu.VMEM((tm, D), BF16)],
        compiler_params=_cparams(("parallel", "arbitrary"), _vmem_limit(est)),
        name="norm_in_proj",
    )(*args)
    return res[0] if w_extra is None else res


def _proj_residual_kernel(*refs, n_lhs):
    lhs_refs = refs[:n_lhs]
    w_ref, x_ref, o_ref = refs[n_lhs:]
    acc = x_ref[...]
    k0 = 0
    for a_ref in lhs_refs:
        k1 = k0 + a_ref.shape[1]
        acc = acc + jnp.dot(a_ref[...], w_ref[k0:k1, :], preferred_element_type=F32)
        k0 = k1
    o_ref[...] = acc


def _proj_residual(lhs_list, w, x, *, layer, tm, tn):
    T, N = x.shape
    K = w.shape[1]
    grid = (T // tm, N // tn)
    in_specs = [pl.BlockSpec((tm, a.shape[1]), lambda i, j: (i, 0)) for a in lhs_list]
    in_specs += [
        pl.BlockSpec((None, K, tn), lambda i, j: (layer, 0, j)),
        pl.BlockSpec((tm, tn), lambda i, j: (i, j)),
    ]
    est = 2 * tm * K * 2 + 2 * K * tn * 2 + 5 * tm * tn * 4
    return pl.pallas_call(
        functools.partial(_proj_residual_kernel, n_lhs=len(lhs_list)),
        grid=grid,
        in_specs=in_specs,
        out_specs=pl.BlockSpec((tm, tn), lambda i, j: (i, j)),
        out_shape=jax.ShapeDtypeStruct((T, N), F32),
        compiler_params=_cparams(("parallel", "arbitrary"), _vmem_limit(est)),
        name="out_proj_residual",
    )(*lhs_list, w, x)


def _ffn_kernel(x_ref, gain_ref, wg_ref, wu_ref, wd_ref, o_ref, h_ref):
    f = pl.program_id(1)

    @pl.when(f == 0)
    def _():
        _rmsnorm_rows(x_ref, gain_ref, h_ref)
        o_ref[...] = x_ref[...]

    h = h_ref[...]
    g = jnp.dot(h, wg_ref[...], preferred_element_type=F32)
    u = jnp.dot(h, wu_ref[...], preferred_element_type=F32)
    a = (_silu(g) * u).astype(BF16)
    n_out = o_ref.shape[1]
    for c0 in range(0, n_out, FFN_OUT_COLS):
        cols = slice(c0, min(c0 + FFN_OUT_COLS, n_out))
        o_ref[:, cols] += jnp.dot(a, wd_ref[:, cols], preferred_element_type=F32)


def _ffn(x, gain, wg, wu, wd, *, layer, tm, tf):
    T, D = x.shape
    F = wg.shape[2]
    grid = (T // tm, F // tf)
    est = 3 * tm * D * 4 + tm * D * 2 + 3 * 2 * D * tf * 2 + 3 * tm * tf * 4 + tm * FFN_OUT_COLS * 4
    return pl.pallas_call(
        _ffn_kernel,
        grid=grid,
        in_specs=[
            pl.BlockSpec((tm, D), lambda i, f: (i, 0), pipeline_mode=pl.Buffered(1)),
            pl.BlockSpec((1, D), lambda i, f: (0, 0)),
            pl.BlockSpec((None, D, tf), lambda i, f: (layer, 0, f)),
            pl.BlockSpec((None, D, tf), lambda i, f: (layer, 0, f)),
            pl.BlockSpec((None, tf, D), lambda i, f: (layer, f, 0)),
        ],
        out_specs=pl.BlockSpec((tm, D), lambda i, f: (i, 0)),
        out_shape=jax.ShapeDtypeStruct((T, D), F32),
        scratch_shapes=[pltpu.VMEM((tm, D), BF16)],
        compiler_params=_cparams(("parallel", "arbitrary"), _vmem_limit(est)),
        name="swiglu_ffn",
    )(x, gain.reshape(1, D), wg, wu, wd)


def _cumsum_rows(a):
    R, D = a.shape
    S8 = V7X_SUBLANES
    a3 = a.reshape(R // S8, S8, D)
    sub = lax.broadcasted_iota(jnp.int32, a3.shape, 1)
    shift = 1
    while shift < S8:
        a3 = a3 + jnp.where(sub >= shift, pltpu.roll(a3, shift, 1), 0.0)
        shift *= 2
    groups = []
    carry = None
    for g in range(R // S8):
        grp = a3[g]
        if carry is not None:
            grp = grp + carry
        groups.append(grp)
        carry = grp[S8 - 1:S8]
    return jnp.concatenate(groups, axis=0)


def _hgrn_level_table(n):
    t = np.arange(n)[:, None]
    s = np.arange(n)[None, :]
    lv = np.floor(np.log2(np.maximum(t ^ s, 1))).astype(np.int32)
    return jnp.asarray(np.where(t > s, lv, np.where(t == s, -1, -2)).astype(np.int32))


def _hgrn_kernel(q_ref, f_ref, i_ref, g_ref, lb_ref, nw_ref, lv_ref, o_ref, st_ref, *, slot):
    @pl.when(pl.program_id(2) == 0)
    def _():
        st_ref[...] = jnp.zeros_like(st_ref)

    for hh in range(q_ref.shape[1] // HG_DIM):
        cols = slice(hh * HG_DIM, (hh + 1) * HG_DIM)
        _hgrn_head(q_ref, f_ref, i_ref, g_ref, lb_ref, nw_ref, lv_ref, o_ref, st_ref.at[hh], cols, slot)


def _hgrn_head(q_ref, f_ref, i_ref, g_ref, lb_ref, nw_ref, lv_ref, o_ref, st_ref, cols, slot):
    C, D = q_ref.shape[0], HG_DIM
    H = lv_ref.shape[0]
    S8 = V7X_SUBLANES

    lbl = lb_ref[:, cols]
    ex = jnp.exp(lbl - jnp.max(lbl, axis=0, keepdims=True))
    lb = jnp.sum(ex[: slot + 1], axis=0, keepdims=True) / jnp.sum(ex, axis=0, keepdims=True)

    th = jnp.tanh(0.5 * f_ref[:, cols])
    q = _silu(q_ref[:, cols])
    log_f = jnp.log(lb + (1.0 - lb) * (0.5 + 0.5 * th))
    k = (1.0 - lb) * (0.5 - 0.5 * th)
    v = i_ref[:, cols]
    v_b = v.astype(BF16)

    b = _cumsum_rows(log_f)
    row = lax.broadcasted_iota(jnp.int32, (C, D), 0)
    b8 = b.reshape(C // S8, S8, D)

    def bcast_row(r):
        return jnp.broadcast_to(b8[:, r:r + 1, :], b8.shape).reshape(C, D)

    lv = lv_ref[...]
    diag = jnp.sum(q * k, axis=1, keepdims=True)
    tiles = [jnp.where(lv == -1, diag[j * H:(j + 1) * H], 0.0) for j in range(C // H)]
    cross = None

    n_levels = int(math.log2(C))
    for lvl in range(n_levels):
        half = 1 << lvl
        blk = 2 * half
        is_q = (row & half) != 0
        if blk == 2:
            beta = jnp.where(is_q, pltpu.roll(b, 1, 0), b)
        elif blk < S8:
            beta = bcast_row(half - 1)
            for r0 in range(blk, S8, blk):
                beta = jnp.where((row & (S8 - 1)) >= r0, bcast_row(r0 + half - 1), beta)
        else:
            mid = b.reshape(C // blk, blk, D)[:, half - 1:half, :]
            beta = jnp.broadcast_to(mid, (C // blk, blk, D)).reshape(C, D)
        x = (jnp.where(is_q, q, k) * jnp.exp(-jnp.abs(b - beta))).astype(BF16)
        if blk <= H:
            for j in range(C // H):
                xj = x[j * H:(j + 1) * H]
                p = lax.dot_general(xj, xj, (((1,), (1,)), ((), ())), preferred_element_type=F32)
                tiles[j] = jnp.where(lv == lvl, p, tiles[j])
        else:
            cross = lax.dot_general(x[H:], x[:H], (((1,), (1,)), ((), ())), preferred_element_type=F32)

    st = st_ref[...]
    o_top = jnp.dot(tiles[0].astype(BF16), v_b[:H], preferred_element_type=F32)
    o_bot = jnp.dot(jnp.concatenate([cross, tiles[1]], axis=1).astype(BF16), v_b, preferred_element_type=F32)
    qe = (q * jnp.exp(b)).astype(BF16)
    o = jnp.concatenate([o_top, o_bot], axis=0) + lax.dot_general(
        qe, st.astype(BF16), (((1,), (1,)), ((), ())), preferred_element_type=F32)

    b_last = b[C - 1:C, :]
    k_dec = (k * jnp.exp(b_last - b)).astype(BF16)
    v_t = v.T.astype(BF16)
    st_ref[...] = st * jnp.exp(b_last) + jnp.dot(v_t, k_dec, preferred_element_type=F32)

    ms = jnp.mean(o * o, axis=-1, keepdims=True)
    o = o * lax.rsqrt(ms + EPS) * nw_ref[...] * _silu(g_ref[:, cols])
    o_ref[:, cols] = o.astype(o_ref.dtype)


def _hgrn(proj, lb_table, norm_w, *, batch, seq, n_heads, col0, slot):
    T = proj.shape[0]
    C = HGRN_CHUNK
    H = C // 2
    nc = seq // C
    hps = HGRN_HEADS_PER_STEP
    wb = hps * HG_DIM
    assert n_heads % hps == 0 and col0 % wb == 0 and (n_heads * HG_DIM) % wb == 0
    n_slots = lb_table.shape[0]

    def sec(s):
        cb = (col0 + s * n_heads * HG_DIM) // wb
        return pl.BlockSpec((C, wb), lambda b, h, c: (b * nc + c, cb + h))

    return pl.pallas_call(
        functools.partial(_hgrn_kernel, slot=slot),
        grid=(batch, n_heads // hps, nc),
        in_specs=[
            sec(0), sec(1), sec(2), sec(3),
            pl.BlockSpec((n_slots, wb), lambda b, h, c: (0, h)),
            pl.BlockSpec((1, HG_DIM), lambda b, h, c: (0, 0)),
            pl.BlockSpec((H, H), lambda b, h, c: (0, 0)),
        ],
        out_specs=pl.BlockSpec((C, wb), lambda b, h, c: (b * nc + c, h)),
        out_shape=jax.ShapeDtypeStruct((T, n_heads * HG_DIM), BF16),
        scratch_shapes=[pltpu.VMEM((hps, HG_DIM, HG_DIM), F32)],
        compiler_params=_cparams(("parallel", "parallel", "arbitrary"), 32 * 1024 * 1024),
        name="hgrn2_scan",
    )(proj, proj, proj, proj, lb_table, norm_w.reshape(1, HG_DIM), _hgrn_level_table(H))


def _moba_kernel(q_ref, k_ref, v_ref, qw_ref, kw_ref, o_ref):
    for hh in range(q_ref.shape[1] // MB_DIM):
        cols = slice(hh * MB_DIM, (hh + 1) * MB_DIM)
        _moba_head(q_ref.at[:, cols], k_ref.at[:, cols], v_ref.at[:, cols], qw_ref, kw_ref, o_ref.at[:, cols])


def _moba_head(q_ref, k_ref, v_ref, qw_ref, kw_ref, o_ref):
    S, D = q_ref.shape
    BLK = MB_BLOCK
    nb = S // BLK
    scale = D ** -0.5

    q = q_ref[...]
    k = k_ref[...]
    qn = q * lax.rsqrt(jnp.mean(q * q, axis=-1, keepdims=True) + EPS) * qw_ref[...]
    kn = k * lax.rsqrt(jnp.mean(k * k, axis=-1, keepdims=True) + EPS) * kw_ref[...]
    k_mean = jnp.mean(kn.reshape(nb, BLK, D), axis=1)
    qn_t = qn.T
    gate_t = jnp.dot(k_mean, qn_t, precision=lax.Precision.HIGHEST,
                     preferred_element_type=F32)
    qs_t = (qn_t * (scale * LOG2_E)).astype(BF16)
    kn_b = kn.astype(BF16)
    v_t = v_ref[...].T.astype(BF16)

    kk = lax.broadcasted_iota(jnp.int32, (BLK, BLK), 0)
    qq = lax.broadcasted_iota(jnp.int32, (BLK, BLK), 1)
    causal = kk <= qq

    for j in range(nb):
        cols = slice(j * BLK, (j + 1) * BLK)
        q_j = qs_t[:, cols]
        n_keep = min(MB_TOPK, j)
        selected = None
        if j > n_keep:
            g = [gate_t[m:m + 1, cols] for m in range(j)]
            selected = []
            for n in range(j):
                beaten = jnp.zeros((1, BLK), jnp.int32)
                for m in range(j):
                    if m < n:
                        beaten = beaten + (g[m] >= g[n]).astype(jnp.int32)
                    elif m > n:
                        beaten = beaten + (g[m] > g[n]).astype(jnp.int32)
                selected.append(beaten < n_keep)
        scores = []
        for n in range(j + 1):
            s = jnp.dot(kn_b[n * BLK:(n + 1) * BLK, :], q_j, preferred_element_type=F32)
            if n == j:
                s = jnp.where(causal, s, NEG_INF)
            elif selected is not None:
                s = jnp.where(selected[n], s, NEG_INF)
            scores.append(s)
        m_run = jnp.max(scores[0], axis=0, keepdims=True)
        for s in scores[1:]:
            m_run = jnp.maximum(m_run, jnp.max(s, axis=0, keepdims=True))
        denom = jnp.zeros((1, BLK), F32)
        acc = jnp.zeros((D, BLK), F32)
        for n, s in enumerate(scores):
            p = jnp.exp2(s - m_run)
            denom = denom + jnp.sum(p, axis=0, keepdims=True)
            acc = acc + jnp.dot(v_t[:, n * BLK:(n + 1) * BLK], p.astype(BF16), preferred_element_type=F32)
        o_ref[j * BLK:(j + 1) * BLK, :] = (acc / denom).T.astype(o_ref.dtype)


def _moba(proj, q_norm_w, k_norm_w, *, batch, seq, n_heads, col0):
    T = proj.shape[0]
    hps = MOBA_HEADS_PER_STEP
    wb = hps * MB_DIM
    assert n_heads % hps == 0 and col0 % wb == 0 and (n_heads * MB_DIM) % wb == 0

    def sec(s):
        cb = (col0 + s * n_heads * MB_DIM) // wb
        return pl.BlockSpec((seq, wb), lambda b, h: (b, cb + h))

    return pl.pallas_call(
        _moba_kernel,
        grid=(batch, n_heads // hps),
        in_specs=[
            sec(0), sec(1), sec(2),
            pl.BlockSpec((1, MB_DIM), lambda b, h: (0, 0)),
            pl.BlockSpec((1, MB_DIM), lambda b, h: (0, 0)),
        ],
        out_specs=pl.BlockSpec((seq, wb), lambda b, h: (b, h)),
        out_shape=jax.ShapeDtypeStruct((T, n_heads * MB_DIM), BF16),
        compiler_params=_cparams(("parallel", "parallel"), 48 * 1024 * 1024),
        name="moba_attention",
    )(proj, proj, proj, q_norm_w.reshape(1, MB_DIM), k_norm_w.reshape(1, MB_DIM))


def _causal_conv_silu(raw, tail, w_ref, bias_ref):
    K = SSM_CONV
    acc = raw * w_ref[K - 1:K, :] + bias_ref[...]
    row8 = lax.broadcasted_iota(jnp.int32, tail.shape, 0)
    for d in range(1, K):
        sh = pltpu.roll(raw, d, 0)
        head = jnp.where(row8 < d, pltpu.roll(tail, d, 0), sh[:V7X_SUBLANES])
        sh = jnp.concatenate([head, sh[V7X_SUBLANES:]], axis=0)
        acc = acc + sh * w_ref[K - 1 - d:K - d, :]
    return _silu(acc)


def _expand_heads(cols, n_heads, width, lane0=0):
    rows = cols.shape[0]
    per_vreg = V7X_LANES // width
    lane = lax.broadcasted_iota(jnp.int32, (rows, V7X_LANES), 1)
    blocks = []
    for h0 in range(lane0, lane0 + n_heads, per_vreg):
        blk = jnp.broadcast_to(cols[:, h0:h0 + 1], (rows, V7X_LANES))
        for k in range(1, per_vreg):
            nxt = jnp.broadcast_to(cols[:, h0 + k:h0 + k + 1], (rows, V7X_LANES))
            blk = jnp.where(lane >= k * width, nxt, blk)
        blocks.append(blk)
    return jnp.concatenate(blocks, axis=1)


def _ssd_kernel(z_ref, x_ref, bm_ref, cm_ref, dt_ref, cwx_ref, cwb_ref, cwc_ref, cbx_ref, cbb_ref, cbc_ref,
                dtb_ref, alog_ref, dsk_ref, nw_ref, o_ref, st_ref, tail_ref, cs2_ref, ecs_ref, wd_ref, key_ref,
                *, group_width):
    L = x_ref.shape[0]
    W, N = group_width, SSM_STATE
    gps = x_ref.shape[1] // W
    all_groups = gps == st_ref.shape[0]

    @pl.when(pl.program_id(2) == 0)
    def _():
        pre = dt_ref[...] + dtb_ref[...]
        dt_all = jnp.maximum(pre, 0.0) + jnp.log1p(jnp.exp(-jnp.abs(pre)))
        a_all = dt_all * (-jnp.exp(alog_ref[...]))
        cs_all = _cumsum_rows(a_all)
        cs2_all = cs_all * LOG2_E
        cs2_ref[...] = cs2_all
        ecs_ref[...] = jnp.exp(cs_all)
        wd_ref[...] = jnp.exp(cs_all[L - 1:L, :] - cs_all) * dt_all
        key_ref[...] = (cs2_all - jnp.log2(dt_all)).T

    for gg in range(gps):
        cw = slice(gg * W, (gg + 1) * W)
        cn = slice(gg * N, (gg + 1) * N)
        _ssd_group(z_ref.at[:, cw], x_ref.at[:, cw], bm_ref.at[:, cn], cm_ref.at[:, cn],
                   cwx_ref.at[:, cw], cwb_ref.at[:, cn], cwc_ref.at[:, cn],
                   cbx_ref.at[:, cw], cbb_ref.at[:, cn], cbc_ref.at[:, cn],
                   dsk_ref, nw_ref.at[:, cw], o_ref.at[:, cw], st_ref, tail_ref, cs2_ref, ecs_ref, wd_ref, key_ref,
                   gg if all_groups else pl.program_id(2) * gps + gg)


def _ssd_group(z_ref, x_ref, bm_ref, cm_ref, cwx_ref, cwb_ref, cwc_ref, cbx_ref, cbb_ref, cbc_ref,
               dsk_ref, nw_ref, o_ref, st_ref, tail_ref, cs2_ref, ecs_ref, wd_ref, key_ref, g):
    L, W = x_ref.shape
    P = SSM_HEADDIM
    hpg = W // P
    N = bm_ref.shape[1]

    @pl.when(pl.program_id(1) == 0)
    def _():
        st_ref[g] = jnp.zeros(st_ref.shape[1:], F32)
        tail_ref[g] = jnp.zeros(tail_ref.shape[1:], F32)

    x_raw = x_ref[...]
    b_raw = bm_ref[...]
    c_raw = cm_ref[...]
    tail = tail_ref[g]
    xs = _causal_conv_silu(x_raw, tail[:, :W], cwx_ref, cbx_ref)
    bm = _causal_conv_silu(b_raw, tail[:, W:W + N], cwb_ref, cbb_ref)
    cm = _causal_conv_silu(c_raw, tail[:, W + N:], cwc_ref, cbc_ref)
    tail_ref[g] = jnp.concatenate(
        [x_raw[L - V7X_SUBLANES:], b_raw[L - V7X_SUBLANES:], c_raw[L - V7X_SUBLANES:]], axis=1)

    if isinstance(g, int):
        head0 = lane0 = g * hpg
        cs2, ecs, wd, dsk = cs2_ref[...], ecs_ref[...], wd_ref[...], dsk_ref[...]
    else:
        head0, lane0 = pl.multiple_of(g * hpg, hpg), 0
        shift = (V7X_LANES - g * hpg) % V7X_LANES
        cs2, ecs, wd, dsk = (pltpu.roll(ref[...], shift, 1) for ref in (cs2_ref, ecs_ref, wd_ref, dsk_ref))
    key_t = key_ref[pl.ds(head0, hpg), :]

    xs_b = xs.astype(BF16)
    bm_b = bm.astype(BF16)
    cm_b = cm.astype(BF16)
    cb = lax.dot_general(cm_b, bm_b, (((1,), (1,)), ((), ())), preferred_element_type=F32)
    TL = V7X_LANES
    n_tiles = L // TL
    lower = (lax.broadcasted_iota(jnp.int32, (TL, TL), 0) >= lax.broadcasted_iota(jnp.int32, (TL, TL), 1))

    def head_rows(r, i):
        t0, s1 = i * TL, (i + 1) * TL
        seg = cs2[t0:s1, lane0 + r:lane0 + r + 1] - key_t[r:r + 1, :s1]
        diag = jnp.exp2(jnp.where(lower, seg[:, t0:], NEG_INF))
        lmat = diag if i == 0 else jnp.concatenate([jnp.exp2(seg[:, :t0]), diag], axis=1)
        return (cb[t0:s1, :s1] * lmat).astype(BF16)

    per_vreg = V7X_LANES // P
    lane_head = lax.broadcasted_iota(jnp.int32, (L, V7X_LANES), 1) // P
    zero = jnp.zeros((L, V7X_LANES), BF16)
    y_blocks = []
    for h0 in range(0, hpg, per_vreg):
        x_blk = xs_b[:, h0 * P:h0 * P + V7X_LANES]
        x_heads = [jnp.where(lane_head == k, x_blk, zero) for k in range(per_vreg)]
        y_rows = []
        for i in range(n_tiles):
            s1 = (i + 1) * TL
            lhs = jnp.concatenate([head_rows(h0 + k, i) for k in range(per_vreg)], axis=1)
            rhs = jnp.concatenate([xh[:s1] for xh in x_heads], axis=0)
            y_rows.append(jnp.dot(lhs, rhs, preferred_element_type=F32))
        y_blocks.append(jnp.concatenate(y_rows, axis=0))
    y = jnp.concatenate(y_blocks, axis=1)

    st = st_ref[g]
    y = y + jnp.dot(cm_b, st.astype(BF16), preferred_element_type=F32) * _expand_heads(ecs, hpg, P, lane0)
    y = y + _expand_heads(dsk, hpg, P, lane0) * xs

    xw = (xs * _expand_heads(wd, hpg, P, lane0)).astype(BF16)
    bm_t = bm.T.astype(BF16)
    st_ref[g] = (st * _expand_heads(ecs[L - 1:L, :], hpg, P, lane0)
                 + jnp.dot(bm_t, xw, preferred_element_type=F32))

    gy = y * _silu(z_ref[...])
    ms = jnp.mean(gy * gy, axis=-1, keepdims=True)
    o_ref[...] = (gy * lax.rsqrt(ms + EPS) * nw_ref[...]).astype(o_ref.dtype)


def _pad_lanes(a):
    return jnp.pad(a.reshape(1, -1), ((0, 0), (0, V7X_LANES - a.shape[0])))


def _ssd(zxbc, dt_raw, conv_w, conv_b, dt_bias, a_log, d_skip, norm_w, *, layer, batch, seq):
    T = zxbc.shape[0]
    G = SSM_GROUPS
    N = SSM_STATE
    heads = dt_bias.shape[1]
    assert heads <= V7X_LANES and dt_raw.shape[1] == V7X_LANES
    inner = heads * SSM_HEADDIM
    W = inner // G
    L = SSD_CHUNK
    nc = seq // L
    gps = SSD_GROUPS_PER_STEP
    assert G % gps == 0
    WB, NB = gps * W, gps * N
    xb, bb, cb_ = inner // WB, (2 * inner) // NB, (2 * inner + G * N) // NB
    wxb, wbb, wcb = 0, inner // NB, (inner + G * N) // NB

    dtb_p = _pad_lanes(dt_bias[layer])
    alog_p = _pad_lanes(a_log[layer])
    dsk_p = _pad_lanes(d_skip[layer])
    conv_b3 = conv_b.reshape(conv_b.shape[0], 1, -1)
    norm_w3 = norm_w.reshape(norm_w.shape[0], 1, -1)

    rows = lambda b, c, g: b * nc + c
    return pl.pallas_call(
        functools.partial(_ssd_kernel, group_width=W),
        grid=(batch, nc, G // gps),
        in_specs=[
            pl.BlockSpec((L, WB), lambda b, c, g: (rows(b, c, g), g)),
            pl.BlockSpec((L, WB), lambda b, c, g: (rows(b, c, g), xb + g)),
            pl.BlockSpec((L, NB), lambda b, c, g: (rows(b, c, g), bb + g)),
            pl.BlockSpec((L, NB), lambda b, c, g: (rows(b, c, g), cb_ + g)),
            pl.BlockSpec((L, V7X_LANES), lambda b, c, g: (rows(b, c, g), 0)),
            pl.BlockSpec((None, SSM_CONV, WB), lambda b, c, g: (layer, 0, wxb + g)),
            pl.BlockSpec((None, SSM_CONV, NB), lambda b, c, g: (layer, 0, wbb + g)),
            pl.BlockSpec((None, SSM_CONV, NB), lambda b, c, g: (layer, 0, wcb + g)),
            pl.BlockSpec((None, 1, WB), lambda b, c, g: (layer, 0, wxb + g)),
            pl.BlockSpec((None, 1, NB), lambda b, c, g: (layer, 0, wbb + g)),
            pl.BlockSpec((None, 1, NB), lambda b, c, g: (layer, 0, wcb + g)),
            pl.BlockSpec((1, V7X_LANES), lambda b, c, g: (0, 0)),
            pl.BlockSpec((1, V7X_LANES), lambda b, c, g: (0, 0)),
            pl.BlockSpec((1, V7X_LANES), lambda b, c, g: (0, 0)),
            pl.BlockSpec((None, 1, WB), lambda b, c, g: (layer, 0, g)),
        ],
        out_specs=pl.BlockSpec((L, WB), lambda b, c, g: (rows(b, c, g), g)),
        out_shape=jax.ShapeDtypeStruct((T, inner), BF16),
        scratch_shapes=[
            pltpu.VMEM((G, N, W), F32),
            pltpu.VMEM((G, V7X_SUBLANES, W + 2 * N), F32),
            pltpu.VMEM((L, V7X_LANES), F32),
            pltpu.VMEM((L, V7X_LANES), F32),
            pltpu.VMEM((L, V7X_LANES), F32),
            pltpu.VMEM((V7X_LANES, L), F32),
        ],
        compiler_params=_cparams(("parallel", "arbitrary", "arbitrary"), 48 * 1024 * 1024),
        name="conv_ssd_scan",
    )(zxbc, zxbc, zxbc, zxbc, dt_raw, conv_w, conv_w, conv_w, conv_b3, conv_b3, conv_b3,
      dtb_p, alog_p, dsk_p, norm_w3)


def kernel(x, ln_mix, ln_ffn, w_in_even, w_out_even, hgrn_lb, hgrn_norm, q_norm, k_norm, w_in_ssm, conv_w,
           conv_b, dt_bias, a_log, d_skip, ssm_norm, w_out_ssm, w_gate, w_up, w_down):
    batch, seq, d_model = x.shape
    depth = ln_mix.shape[0]
    T = batch * seq
    xf = x.reshape(T, d_model)

    hg_width = hgrn_lb.shape[1]
    hg_heads = hg_width // HG_DIM
    mb_width = (w_in_even.shape[2] - 4 * hg_width) // 3
    mb_heads = mb_width // MB_DIM
    ssm_heads = dt_bias.shape[1]
    ssm_main = w_in_ssm.shape[2] - ssm_heads

    w_in_even_b = w_in_even.astype(BF16)
    w_out_even_b = w_out_even.astype(BF16)
    w_in_ssm_b = w_in_ssm.astype(BF16)
    w_out_ssm_b = w_out_ssm.astype(BF16)
    w_gate_b = w_gate.astype(BF16)
    w_up_b = w_up.astype(BF16)
    w_down_b = w_down.astype(BF16)

    tm = min(1024, T)
    for layer in range(depth):
        if layer % 2 == 0:
            e = layer // 2
            proj = _norm_matmul(xf, ln_mix[layer], w_in_even_b, layer=e, n_cols=w_in_even.shape[2], tm=tm, tn=1024)
            o_a = _hgrn(proj, hgrn_lb, hgrn_norm[e], batch=batch, seq=seq, n_heads=hg_heads, col0=0, slot=e)
            o_b = _moba(proj, q_norm[e], k_norm[e], batch=batch, seq=seq, n_heads=mb_heads, col0=4 * hg_width)
            xf = _proj_residual([o_a, o_b], w_out_even_b, xf, layer=e, tm=tm, tn=512)
        else:
            o = layer // 2
            w_dt = jnp.pad(w_in_ssm[o][:, ssm_main:], ((0, 0), (0, V7X_LANES - ssm_heads))).astype(BF16)
            zxbc, dt_raw = _norm_matmul(xf, ln_mix[layer], w_in_ssm_b, w_dt, layer=o, n_cols=ssm_main, tm=tm, tn=1024)
            gy = _ssd(zxbc, dt_raw, conv_w, conv_b, dt_bias, a_log, d_skip, ssm_norm, layer=o, batch=batch, seq=seq)
            xf = _proj_residual([gy], w_out_ssm_b, xf, layer=o, tm=tm, tn=512)
        xf = _ffn(xf, ln_ffn[layer], w_gate_b, w_up_b, w_down_b, layer=layer, tm=tm, tf=512)
    return xf.reshape(batch, seq, d_model)
```

```python
import functools
import math

import jax
import jax.numpy as jnp
import numpy as np
from jax import lax
from jax.experimental import pallas as pl
from jax.experimental.pallas import tpu as pltpu

F32 = jnp.float32
BF16 = jnp.bfloat16
EPS = 1e-6
NEG_INF = float("-inf")
LOG2_E = 1.4426950408889634

HG_DIM = 128
MB_DIM = 128
MB_BLOCK = 256
MB_TOPK = 3
SSM_HEADDIM = 64
SSM_STATE = 128
SSM_GROUPS = 8
SSM_CONV = 4

V7X_LANES = 128
V7X_SUBLANES = 8
V7X_VMEM_BYTES = 64 * 1024 * 1024

HGRN_CHUNK = 256
HGRN_HEADS_PER_STEP = 8
SSD_GROUPS_PER_STEP = 8
MOBA_HEADS_PER_STEP = 2
SSD_CHUNK = 256
NORM_ROWS = 128
FFN_OUT_COLS = 512


def _cparams(semantics, vmem_bytes):
    return pltpu.CompilerParams(dimension_semantics=semantics, vmem_limit_bytes=int(vmem_bytes))


def _vmem_limit(estimate_bytes):
    return min(V7X_VMEM_BYTES - 6 * 1024 * 1024, max(32 * 1024 * 1024, int(estimate_bytes * 1.25)))


def _silu(x):
    h = 0.5 * x
    return h + h * jnp.tanh(h)


def _rmsnorm_rows(x_ref, gain_ref, h_ref):
    n_steps = x_ref.shape[0] // NORM_ROWS

    def body(r, carry):
        rows = pl.ds(pl.multiple_of(r * NORM_ROWS, NORM_ROWS), NORM_ROWS)
        x = x_ref[rows, :]
        ms = jnp.mean(x * x, axis=-1, keepdims=True)
        h_ref[rows, :] = (x * lax.rsqrt(ms + EPS) * gain_ref[...]).astype(h_ref.dtype)
        return carry

    lax.fori_loop(0, n_steps, body, 0)


def _norm_matmul_kernel(x_ref, gain_ref, w_ref, o_ref, h_ref):
    @pl.when(pl.program_id(1) == 0)
    def _():
        _rmsnorm_rows(x_ref, gain_ref, h_ref)

    o_ref[...] = jnp.dot(h_ref[...], w_ref[...], preferred_element_type=F32).astype(o_ref.dtype)


def _norm_matmul_extra_kernel(x_ref, gain_ref, w_ref, w2_ref, o_ref, o2_ref, h_ref):
    @pl.when(pl.program_id(1) == 0)
    def _():
        _rmsnorm_rows(x_ref, gain_ref, h_ref)
        o2_ref[...] = jnp.dot(h_ref[...], w2_ref[...], preferred_element_type=F32)

    o_ref[...] = jnp.dot(h_ref[...], w_ref[...], preferred_element_type=F32).astype(o_ref.dtype)


def _norm_matmul(x, gain, w, w_extra=None, *, layer, n_cols, tm, tn):
    T, D = x.shape
    N = n_cols
    grid = (T // tm, N // tn)
    in_specs = [
        pl.BlockSpec((tm, D), lambda i, j: (i, 0)),
        pl.BlockSpec((1, D), lambda i, j: (0, 0)),
        pl.BlockSpec((None, D, tn), lambda i, j: (layer, 0, j)),
    ]
    out_shape = [jax.ShapeDtypeStruct((T, N), F32)]
    out_specs = [pl.BlockSpec((tm, tn), lambda i, j: (i, j))]
    args = [x, gain.reshape(1, D), w]
    est = 2 * tm * D * 4 + tm * D * 2 + 2 * D * tn * 2 + 3 * tm * tn * 4
    if w_extra is None:
        body = _norm_matmul_kernel
    else:
        body = _norm_matmul_extra_kernel
        n2 = w_extra.shape[1]
        in_specs.append(pl.BlockSpec((D, n2), lambda i, j: (0, 0)))
        out_shape.append(jax.ShapeDtypeStruct((T, n2), F32))
        out_specs.append(pl.BlockSpec((tm, n2), lambda i, j: (i, 0)))
        args.append(w_extra)
        est += 2 * D * n2 * 2 + 2 * tm * n2 * 4
    res = pl.pallas_call(
        body,
        grid=grid,
        in_specs=in_specs,
        out_specs=out_specs,
        out_shape=out_shape,
        scratch_shapes=[pltpu.VMEM((tm, D), BF16)],
        compiler_params=_cparams(("parallel", "arbitrary"), _vmem_limit(est)),
        name="norm_in_proj",
    )(*args)
    return res[0] if w_extra is None else res


def _proj_residual_kernel(*refs, n_lhs):
    lhs_refs = refs[:n_lhs]
    w_ref, x_ref, o_ref = refs[n_lhs:]
    acc = x_ref[...]
    k0 = 0
    for a_ref in lhs_refs:
        k1 = k0 + a_ref.shape[1]
        acc = acc + jnp.dot(a_ref[...], w_ref[k0:k1, :], preferred_element_type=F32)
        k0 = k1
    o_ref[...] = acc


def _proj_residual(lhs_list, w, x, *, layer, tm, tn):
    T, N = x.shape
    K = w.shape[1]
    grid = (T // tm, N // tn)
    in_specs = [pl.BlockSpec((tm, a.shape[1]), lambda i, j: (i, 0)) for a in lhs_list]
    in_specs += [
        pl.BlockSpec((None, K, tn), lambda i, j: (layer, 0, j)),
        pl.BlockSpec((tm, tn), lambda i, j: (i, j)),
    ]
    est = 2 * tm * K * 2 + 2 * K * tn * 2 + 5 * tm * tn * 4
    return pl.pallas_call(
        functools.partial(_proj_residual_kernel, n_lhs=len(lhs_list)),
        grid=grid,
        in_specs=in_specs,
        out_specs=pl.BlockSpec((tm, tn), lambda i, j: (i, j)),
        out_shape=jax.ShapeDtypeStruct((T, N), F32),
        compiler_params=_cparams(("parallel", "arbitrary"), _vmem_limit(est)),
        name="out_proj_residual",
    )(*lhs_list, w, x)


def _ffn_kernel(x_ref, gain_ref, wg_ref, wu_ref, wd_ref, o_ref, h_ref):
    f = pl.program_id(1)

    @pl.when(f == 0)
    def _():
        _rmsnorm_rows(x_ref, gain_ref, h_ref)

    h = h_ref[...]
    g = jnp.dot(h, wg_ref[...], preferred_element_type=F32)
    u = jnp.dot(h, wu_ref[...], preferred_element_type=F32)
    a = (_silu(g) * u).astype(BF16)
    n_out = o_ref.shape[1]

    def down(c0):
        cols = slice(c0, min(c0 + FFN_OUT_COLS, n_out))
        return cols, jnp.dot(a, wd_ref[:, cols], preferred_element_type=F32)

    @pl.when(f == 0)
    def _():
        for c0 in range(0, n_out, FFN_OUT_COLS):
            cols, d = down(c0)
            o_ref[:, cols] = x_ref[:, cols] + d

    @pl.when(f != 0)
    def _():
        for c0 in range(0, n_out, FFN_OUT_COLS):
            cols, d = down(c0)
            o_ref[:, cols] += d


def _ffn(x, gain, wg, wu, wd, *, layer, tm, tf):
    T, D = x.shape
    F = wg.shape[2]
    grid = (T // tm, F // tf)
    est = 4 * tm * D * 4 + tm * D * 2 + 3 * 2 * D * tf * 2 + 3 * tm * tf * 4 + tm * FFN_OUT_COLS * 4
    return pl.pallas_call(
        _ffn_kernel,
        grid=grid,
        in_specs=[
            pl.BlockSpec((tm, D), lambda i, f: (i, 0)),
            pl.BlockSpec((1, D), lambda i, f: (0, 0)),
            pl.BlockSpec((None, D, tf), lambda i, f: (layer, 0, f)),
            pl.BlockSpec((None, D, tf), lambda i, f: (layer, 0, f)),
            pl.BlockSpec((None, tf, D), lambda i, f: (layer, f, 0)),
        ],
        out_specs=pl.BlockSpec((tm, D), lambda i, f: (i, 0)),
        out_shape=jax.ShapeDtypeStruct((T, D), F32),
        scratch_shapes=[pltpu.VMEM((tm, D), BF16)],
        compiler_params=_cparams(("parallel", "arbitrary"), _vmem_limit(est)),
        name="swiglu_ffn",
    )(x, gain.reshape(1, D), wg, wu, wd)


def _cumsum_rows(a):
    R, D = a.shape
    S8 = V7X_SUBLANES
    a3 = a.reshape(R // S8, S8, D)
    sub = lax.broadcasted_iota(jnp.int32, a3.shape, 1)
    shift = 1
    while shift < S8:
        a3 = a3 + jnp.where(sub >= shift, pltpu.roll(a3, shift, 1), 0.0)
        shift *= 2
    groups = []
    carry = None
    for g in range(R // S8):
        grp = a3[g]
        if carry is not None:
            grp = grp + carry
        groups.append(grp)
        carry = grp[S8 - 1:S8]
    return jnp.concatenate(groups, axis=0)


def _hgrn_level_table(n):
    t = np.arange(n)[:, None]
    s = np.arange(n)[None, :]
    lv = np.floor(np.log2(np.maximum(t ^ s, 1))).astype(np.int32)
    return jnp.asarray(np.where(t > s, lv, np.where(t == s, -1, -2)).astype(np.int32))


def _hgrn_kernel(q_ref, f_ref, i_ref, g_ref, lb_ref, nw_ref, lv_ref, o_ref, st_ref, *, slot):
    @pl.when(pl.program_id(2) == 0)
    def _():
        st_ref[...] = jnp.zeros_like(st_ref)

    for hh in range(q_ref.shape[1] // HG_DIM):
        cols = slice(hh * HG_DIM, (hh + 1) * HG_DIM)
        _hgrn_head(q_ref, f_ref, i_ref, g_ref, lb_ref, nw_ref, lv_ref, o_ref, st_ref.at[hh], cols, slot)


def _hgrn_head(q_ref, f_ref, i_ref, g_ref, lb_ref, nw_ref, lv_ref, o_ref, st_ref, cols, slot):
    C, D = q_ref.shape[0], HG_DIM
    H = lv_ref.shape[0]
    S8 = V7X_SUBLANES

    lbl = lb_ref[:, cols]
    ex = jnp.exp(lbl - jnp.max(lbl, axis=0, keepdims=True))
    lb = jnp.sum(ex[: slot + 1], axis=0, keepdims=True) / jnp.sum(ex, axis=0, keepdims=True)

    th = jnp.tanh(0.5 * f_ref[:, cols])
    q = _silu(q_ref[:, cols])
    log_f = jnp.log(lb + (1.0 - lb) * (0.5 + 0.5 * th))
    k = (1.0 - lb) * (0.5 - 0.5 * th)
    v = i_ref[:, cols]
    v_b = v.astype(BF16)

    b = _cumsum_rows(log_f) * LOG2_E
    row = lax.broadcasted_iota(jnp.int32, (C, D), 0)
    b8 = b.reshape(C // S8, S8, D)

    def bcast_row(r):
        return jnp.broadcast_to(b8[:, r:r + 1, :], b8.shape).reshape(C, D)

    lv = lv_ref[...]
    diag = jnp.sum(q * k, axis=1, keepdims=True)
    tiles = [jnp.where(lv == -1, diag[j * H:(j + 1) * H], 0.0) for j in range(C // H)]
    cross = None

    n_levels = int(math.log2(C))
    for lvl in range(n_levels):
        half = 1 << lvl
        blk = 2 * half
        is_q = (row & half) != 0
        if blk == 2:
            beta = jnp.where(is_q, pltpu.roll(b, 1, 0), b)
        elif blk < S8:
            beta = bcast_row(half - 1)
            for r0 in range(blk, S8, blk):
                beta = jnp.where((row & (S8 - 1)) >= r0, bcast_row(r0 + half - 1), beta)
        else:
            mid = b.reshape(C // blk, blk, D)[:, half - 1:half, :]
            beta = jnp.broadcast_to(mid, (C // blk, blk, D)).reshape(C, D)
        x = (jnp.where(is_q, q, k) * jnp.exp2(-jnp.abs(b - beta))).astype(BF16)
        if blk <= H:
            for j in range(C // H):
                xj = x[j * H:(j + 1) * H]
                p = lax.dot_general(xj, xj, (((1,), (1,)), ((), ())), preferred_element_type=F32)
                tiles[j] = jnp.where(lv == lvl, p, tiles[j])
        else:
            cross = lax.dot_general(x[H:], x[:H], (((1,), (1,)), ((), ())), preferred_element_type=F32)

    st = st_ref[...]
    o_top = jnp.dot(tiles[0].astype(BF16), v_b[:H], preferred_element_type=F32)
    o_bot = jnp.dot(jnp.concatenate([cross, tiles[1]], axis=1).astype(BF16), v_b, preferred_element_type=F32)
    qe = (q * jnp.exp2(b)).astype(BF16)
    o = jnp.concatenate([o_top, o_bot], axis=0) + lax.dot_general(
        qe, st.astype(BF16), (((1,), (1,)), ((), ())), preferred_element_type=F32)

    b_last = b[C - 1:C, :]
    k_dec = (k * jnp.exp2(b_last - b)).astype(BF16)
    v_t = v.T.astype(BF16)
    st_ref[...] = st * jnp.exp2(b_last) + jnp.dot(v_t, k_dec, preferred_element_type=F32)

    ms = jnp.mean(o * o, axis=-1, keepdims=True)
    o = o * lax.rsqrt(ms + EPS) * nw_ref[...] * _silu(g_ref[:, cols])
    o_ref[:, cols] = o.astype(o_ref.dtype)


def _hgrn(proj, lb_table, norm_w, *, batch, seq, n_heads, col0, slot):
    T = proj.shape[0]
    C = HGRN_CHUNK
    H = C // 2
    nc = seq // C
    hps = HGRN_HEADS_PER_STEP
    wb = hps * HG_DIM
    assert n_heads % hps == 0 and col0 % wb == 0 and (n_heads * HG_DIM) % wb == 0
    n_slots = lb_table.shape[0]

    def sec(s):
        cb = (col0 + s * n_heads * HG_DIM) // wb
        return pl.BlockSpec((C, wb), lambda b, h, c: (b * nc + c, cb + h))

    return pl.pallas_call(
        functools.partial(_hgrn_kernel, slot=slot),
        grid=(batch, n_heads // hps, nc),
        in_specs=[
            sec(0), sec(1), sec(2), sec(3),
            pl.BlockSpec((n_slots, wb), lambda b, h, c: (0, h)),
            pl.BlockSpec((1, HG_DIM), lambda b, h, c: (0, 0)),
            pl.BlockSpec((H, H), lambda b, h, c: (0, 0)),
        ],
        out_specs=pl.BlockSpec((C, wb), lambda b, h, c: (b * nc + c, h)),
        out_shape=jax.ShapeDtypeStruct((T, n_heads * HG_DIM), BF16),
        scratch_shapes=[pltpu.VMEM((hps, HG_DIM, HG_DIM), F32)],
        compiler_params=_cparams(("parallel", "parallel", "arbitrary"), 32 * 1024 * 1024),
        name="hgrn2_scan",
    )(proj, proj, proj, proj, lb_table, norm_w.reshape(1, HG_DIM), _hgrn_level_table(H))


def _moba_kernel(q_ref, k_ref, v_ref, qw_ref, kw_ref, o_ref):
    for hh in range(q_ref.shape[1] // MB_DIM):
        cols = slice(hh * MB_DIM, (hh + 1) * MB_DIM)
        _moba_head(q_ref.at[:, cols], k_ref.at[:, cols], v_ref.at[:, cols], qw_ref, kw_ref, o_ref.at[:, cols])


def _moba_head(q_ref, k_ref, v_ref, qw_ref, kw_ref, o_ref):
    S, D = q_ref.shape
    BLK = MB_BLOCK
    nb = S // BLK
    scale = D ** -0.5

    q = q_ref[...]
    k = k_ref[...]
    qn = q * lax.rsqrt(jnp.mean(q * q, axis=-1, keepdims=True) + EPS) * qw_ref[...]
    kn = k * lax.rsqrt(jnp.mean(k * k, axis=-1, keepdims=True) + EPS) * kw_ref[...]
    k_mean = jnp.mean(kn.reshape(nb, BLK, D), axis=1)
    qn_t = qn.T
    gate_t = jnp.dot(k_mean, qn_t, precision=lax.Precision.HIGHEST,
                     preferred_element_type=F32)
    qs_t = (qn_t * (scale * LOG2_E)).astype(BF16)
    kn_b = kn.astype(BF16)
    v_t = v_ref[...].T.astype(BF16)

    kk = lax.broadcasted_iota(jnp.int32, (BLK, BLK), 0)
    qq = lax.broadcasted_iota(jnp.int32, (BLK, BLK), 1)
    causal = kk <= qq

    for j in range(nb):
        cols = slice(j * BLK, (j + 1) * BLK)
        q_j = qs_t[:, cols]
        n_keep = min(MB_TOPK, j)
        selected = None
        if j > n_keep:
            g = [gate_t[m:m + 1, cols] for m in range(j)]
            selected = []
            for n in range(j):
                beaten = jnp.zeros((1, BLK), jnp.int32)
                for m in range(j):
                    if m < n:
                        beaten = beaten + (g[m] >= g[n]).astype(jnp.int32)
                    elif m > n:
                        beaten = beaten + (g[m] > g[n]).astype(jnp.int32)
                selected.append(beaten < n_keep)
        scores = []
        for n in range(j + 1):
            s = jnp.dot(kn_b[n * BLK:(n + 1) * BLK, :], q_j, preferred_element_type=F32)
            if n == j:
                s = jnp.where(causal, s, NEG_INF)
            elif selected is not None:
                s = jnp.where(selected[n], s, NEG_INF)
            scores.append(s)
        m_run = jnp.max(scores[0], axis=0, keepdims=True)
        for s in scores[1:]:
            m_run = jnp.maximum(m_run, jnp.max(s, axis=0, keepdims=True))
        denom = jnp.zeros((1, BLK), F32)
        acc = jnp.zeros((D, BLK), F32)
        for n, s in enumerate(scores):
            p = jnp.exp2(s - m_run)
            denom = denom + jnp.sum(p, axis=0, keepdims=True)
            acc = acc + jnp.dot(v_t[:, n * BLK:(n + 1) * BLK], p.astype(BF16), preferred_element_type=F32)
        o_ref[j * BLK:(j + 1) * BLK, :] = (acc / denom).T.astype(o_ref.dtype)


def _moba(proj, q_norm_w, k_norm_w, *, batch, seq, n_heads, col0):
    T = proj.shape[0]
    hps = MOBA_HEADS_PER_STEP
    wb = hps * MB_DIM
    assert n_heads % hps == 0 and col0 % wb == 0 and (n_heads * MB_DIM) % wb == 0

    def sec(s):
        cb = (col0 + s * n_heads * MB_DIM) // wb
        return pl.BlockSpec((seq, wb), lambda b, h: (b, cb + h))

    return pl.pallas_call(
        _moba_kernel,
        grid=(batch, n_heads // hps),
        in_specs=[
            sec(0), sec(1), sec(2),
            pl.BlockSpec((1, MB_DIM), lambda b, h: (0, 0)),
            pl.BlockSpec((1, MB_DIM), lambda b, h: (0, 0)),
        ],
        out_specs=pl.BlockSpec((seq, wb), lambda b, h: (b, h)),
        out_shape=jax.ShapeDtypeStruct((T, n_heads * MB_DIM), BF16),
        compiler_params=_cparams(("parallel", "parallel"), 48 * 1024 * 1024),
        name="moba_attention",
    )(proj, proj, proj, q_norm_w.reshape(1, MB_DIM), k_norm_w.reshape(1, MB_DIM))


def _causal_conv_silu(raw, tail, w_ref, bias_ref):
    K = SSM_CONV
    acc = raw * w_ref[K - 1:K, :] + bias_ref[...]
    row8 = lax.broadcasted_iota(jnp.int32, tail.shape, 0)
    for d in range(1, K):
        sh = pltpu.roll(raw, d, 0)
        head = jnp.where(row8 < d, pltpu.roll(tail, d, 0), sh[:V7X_SUBLANES])
        sh = jnp.concatenate([head, sh[V7X_SUBLANES:]], axis=0)
        acc = acc + sh * w_ref[K - 1 - d:K - d, :]
    return _silu(acc)


def _expand_heads(cols, n_heads, width, lane0=0):
    rows = cols.shape[0]
    per_vreg = V7X_LANES // width
    lane = lax.broadcasted_iota(jnp.int32, (rows, V7X_LANES), 1)
    blocks = []
    for h0 in range(lane0, lane0 + n_heads, per_vreg):
        blk = jnp.broadcast_to(cols[:, h0:h0 + 1], (rows, V7X_LANES))
        for k in range(1, per_vreg):
            nxt = jnp.broadcast_to(cols[:, h0 + k:h0 + k + 1], (rows, V7X_LANES))
            blk = jnp.where(lane >= k * width, nxt, blk)
        blocks.append(blk)
    return jnp.concatenate(blocks, axis=1)


def _ssd_kernel(z_ref, x_ref, bm_ref, cm_ref, dt_ref, cwx_ref, cwb_ref, cwc_ref, cbx_ref, cbb_ref, cbc_ref,
                dtb_ref, alog_ref, dsk_ref, nw_ref, o_ref, st_ref, tail_ref, cs2_ref, ecs_ref, wd_ref, key_ref,
                *, group_width):
    L = x_ref.shape[0]
    W, N = group_width, SSM_STATE
    gps = x_ref.shape[1] // W
    all_groups = gps == st_ref.shape[0]

    @pl.when(pl.program_id(2) == 0)
    def _():
        pre = dt_ref[...] + dtb_ref[...]
        dt_all = jnp.maximum(pre, 0.0) + jnp.log1p(jnp.exp(-jnp.abs(pre)))
        a_all = dt_all * (-jnp.exp(alog_ref[...]))
        cs_all = _cumsum_rows(a_all)
        cs2_all = cs_all * LOG2_E
        cs2_ref[...] = cs2_all
        ecs_ref[...] = jnp.exp(cs_all)
        wd_ref[...] = jnp.exp(cs_all[L - 1:L, :] - cs_all) * dt_all
        key_ref[...] = (cs2_all - jnp.log2(dt_all)).T

    for gg in range(gps):
        cw = slice(gg * W, (gg + 1) * W)
        cn = slice(gg * N, (gg + 1) * N)
        _ssd_group(z_ref.at[:, cw], x_ref.at[:, cw], bm_ref.at[:, cn], cm_ref.at[:, cn],
                   cwx_ref.at[:, cw], cwb_ref.at[:, cn], cwc_ref.at[:, cn],
                   cbx_ref.at[:, cw], cbb_ref.at[:, cn], cbc_ref.at[:, cn],
                   dsk_ref, nw_ref.at[:, cw], o_ref.at[:, cw], st_ref, tail_ref, cs2_ref, ecs_ref, wd_ref, key_ref,
                   gg if all_groups else pl.program_id(2) * gps + gg)


def _ssd_group(z_ref, x_ref, bm_ref, cm_ref, cwx_ref, cwb_ref, cwc_ref, cbx_ref, cbb_ref, cbc_ref,
               dsk_ref, nw_ref, o_ref, st_ref, tail_ref, cs2_ref, ecs_ref, wd_ref, key_ref, g):
    L, W = x_ref.shape
    P = SSM_HEADDIM
    hpg = W // P
    N = bm_ref.shape[1]

    @pl.when(pl.program_id(1) == 0)
    def _():
        st_ref[g] = jnp.zeros(st_ref.shape[1:], F32)
        tail_ref[g] = jnp.zeros(tail_ref.shape[1:], F32)

    x_raw = x_ref[...]
    b_raw = bm_ref[...]
    c_raw = cm_ref[...]
    tail = tail_ref[g]
    xs = _causal_conv_silu(x_raw, tail[:, :W], cwx_ref, cbx_ref)
    bm = _causal_conv_silu(b_raw, tail[:, W:W + N], cwb_ref, cbb_ref)
    cm = _causal_conv_silu(c_raw, tail[:, W + N:], cwc_ref, cbc_ref)
    tail_ref[g] = jnp.concatenate(
        [x_raw[L - V7X_SUBLANES:], b_raw[L - V7X_SUBLANES:], c_raw[L - V7X_SUBLANES:]], axis=1)

    if isinstance(g, int):
        head0 = lane0 = g * hpg
        cs2, ecs, wd, dsk = cs2_ref[...], ecs_ref[...], wd_ref[...], dsk_ref[...]
    else:
        head0, lane0 = pl.multiple_of(g * hpg, hpg), 0
        shift = (V7X_LANES - g * hpg) % V7X_LANES
        cs2, ecs, wd, dsk = (pltpu.roll(ref[...], shift, 1) for ref in (cs2_ref, ecs_ref, wd_ref, dsk_ref))
    key_t = key_ref[pl.ds(head0, hpg), :]

    xs_b = xs.astype(BF16)
    bm_b = bm.astype(BF16)
    cm_b = cm.astype(BF16)
    cb = lax.dot_general(cm_b, bm_b, (((1,), (1,)), ((), ())), preferred_element_type=F32)
    TL = V7X_LANES
    n_tiles = L // TL
    lower = (lax.broadcasted_iota(jnp.int32, (TL, TL), 0) >= lax.broadcasted_iota(jnp.int32, (TL, TL), 1))

    def head_rows(r, i):
        t0, s1 = i * TL, (i + 1) * TL
        seg = cs2[t0:s1, lane0 + r:lane0 + r + 1] - key_t[r:r + 1, :s1]
        diag = jnp.exp2(jnp.where(lower, seg[:, t0:], NEG_INF))
        lmat = diag if i == 0 else jnp.concatenate([jnp.exp2(seg[:, :t0]), diag], axis=1)
        return (cb[t0:s1, :s1] * lmat).astype(BF16)

    per_vreg = V7X_LANES // P
    lane_head = lax.broadcasted_iota(jnp.int32, (L, V7X_LANES), 1) // P
    zero = jnp.zeros((L, V7X_LANES), BF16)
    y_blocks = []
    for h0 in range(0, hpg, per_vreg):
        x_blk = xs_b[:, h0 * P:h0 * P + V7X_LANES]
        x_heads = [jnp.where(lane_head == k, x_blk, zero) for k in range(per_vreg)]
        y_rows = []
        for i in range(n_tiles):
            s1 = (i + 1) * TL
            lhs = jnp.concatenate([head_rows(h0 + k, i) for k in range(per_vreg)], axis=1)
            rhs = jnp.concatenate([xh[:s1] for xh in x_heads], axis=0)
            y_rows.append(jnp.dot(lhs, rhs, preferred_element_type=F32))
        y_blocks.append(jnp.concatenate(y_rows, axis=0))
    y = jnp.concatenate(y_blocks, axis=1)

    st = st_ref[g]
    y = y + jnp.dot(cm_b, st.astype(BF16), preferred_element_type=F32) * _expand_heads(ecs, hpg, P, lane0)
    y = y + _expand_heads(dsk, hpg, P, lane0) * xs

    xw = (xs * _expand_heads(wd, hpg, P, lane0)).astype(BF16)
    bm_t = bm.T.astype(BF16)
    st_ref[g] = (st * _expand_heads(ecs[L - 1:L, :], hpg, P, lane0)
                 + jnp.dot(bm_t, xw, preferred_element_type=F32))

    gy = y * _silu(z_ref[...])
    ms = jnp.mean(gy * gy, axis=-1, keepdims=True)
    o_ref[...] = (gy * lax.rsqrt(ms + EPS) * nw_ref[...]).astype(o_ref.dtype)


def _pad_lanes(a):
    return jnp.pad(a.reshape(1, -1), ((0, 0), (0, V7X_LANES - a.shape[0])))


def _ssd(zxbc, dt_raw, conv_w, conv_b, dt_bias, a_log, d_skip, norm_w, *, layer, batch, seq):
    T = zxbc.shape[0]
    G = SSM_GROUPS
    N = SSM_STATE
    heads = dt_bias.shape[1]
    assert heads <= V7X_LANES and dt_raw.shape[1] == V7X_LANES
    inner = heads * SSM_HEADDIM
    W = inner // G
    L = SSD_CHUNK
    nc = seq // L
    gps = SSD_GROUPS_PER_STEP
    assert G % gps == 0
    WB, NB = gps * W, gps * N
    xb, bb, cb_ = inner // WB, (2 * inner) // NB, (2 * inner + G * N) // NB
    wxb, wbb, wcb = 0, inner // NB, (inner + G * N) // NB

    dtb_p = _pad_lanes(dt_bias[layer])
    alog_p = _pad_lanes(a_log[layer])
    dsk_p = _pad_lanes(d_skip[layer])
    conv_b3 = conv_b.reshape(conv_b.shape[0], 1, -1)
    norm_w3 = norm_w.reshape(norm_w.shape[0], 1, -1)

    rows = lambda b, c, g: b * nc + c
    return pl.pallas_call(
        functools.partial(_ssd_kernel, group_width=W),
        grid=(batch, nc, G // gps),
        in_specs=[
            pl.BlockSpec((L, WB), lambda b, c, g: (rows(b, c, g), g)),
            pl.BlockSpec((L, WB), lambda b, c, g: (rows(b, c, g), xb + g)),
            pl.BlockSpec((L, NB), lambda b, c, g: (rows(b, c, g), bb + g)),
            pl.BlockSpec((L, NB), lambda b, c, g: (rows(b, c, g), cb_ + g)),
            pl.BlockSpec((L, V7X_LANES), lambda b, c, g: (rows(b, c, g), 0)),
            pl.BlockSpec((None, SSM_CONV, WB), lambda b, c, g: (layer, 0, wxb + g)),
            pl.BlockSpec((None, SSM_CONV, NB), lambda b, c, g: (layer, 0, wbb + g)),
            pl.BlockSpec((None, SSM_CONV, NB), lambda b, c, g: (layer, 0, wcb + g)),
            pl.BlockSpec((None, 1, WB), lambda b, c, g: (layer, 0, wxb + g)),
            pl.BlockSpec((None, 1, NB), lambda b, c, g: (layer, 0, wbb + g)),
            pl.BlockSpec((None, 1, NB), lambda b, c, g: (layer, 0, wcb + g)),
            pl.BlockSpec((1, V7X_LANES), lambda b, c, g: (0, 0)),
            pl.BlockSpec((1, V7X_LANES), lambda b, c, g: (0, 0)),
            pl.BlockSpec((1, V7X_LANES), lambda b, c, g: (0, 0)),
            pl.BlockSpec((None, 1, WB), lambda b, c, g: (layer, 0, g)),
        ],
        out_specs=pl.BlockSpec((L, WB), lambda b, c, g: (rows(b, c, g), g)),
        out_shape=jax.ShapeDtypeStruct((T, inner), BF16),
        scratch_shapes=[
            pltpu.VMEM((G, N, W), F32),
            pltpu.VMEM((G, V7X_SUBLANES, W + 2 * N), F32),
            pltpu.VMEM((L, V7X_LANES), F32),
            pltpu.VMEM((L, V7X_LANES), F32),
            pltpu.VMEM((L, V7X_LANES), F32),
            pltpu.VMEM((V7X_LANES, L), F32),
        ],
        compiler_params=_cparams(("parallel", "arbitrary", "arbitrary"), 48 * 1024 * 1024),
        name="conv_ssd_scan",
    )(zxbc, zxbc, zxbc, zxbc, dt_raw, conv_w, conv_w, conv_w, conv_b3, conv_b3, conv_b3,
      dtb_p, alog_p, dsk_p, norm_w3)


def kernel(x, ln_mix, ln_ffn, w_in_even, w_out_even, hgrn_lb, hgrn_norm, q_norm, k_norm, w_in_ssm, conv_w,
           conv_b, dt_bias, a_log, d_skip, ssm_norm, w_out_ssm, w_gate, w_up, w_down):
    batch, seq, d_model = x.shape
    depth = ln_mix.shape[0]
    T = batch * seq
    xf = x.reshape(T, d_model)

    hg_width = hgrn_lb.shape[1]
    hg_heads = hg_width // HG_DIM
    mb_width = (w_in_even.shape[2] - 4 * hg_width) // 3
    mb_heads = mb_width // MB_DIM
    ssm_heads = dt_bias.shape[1]
    ssm_main = w_in_ssm.shape[2] - ssm_heads

    w_in_even_b = w_in_even.astype(BF16)
    w_out_even_b = w_out_even.astype(BF16)
    w_in_ssm_b = w_in_ssm.astype(BF16)
    w_out_ssm_b = w_out_ssm.astype(BF16)
    w_gate_b = w_gate.astype(BF16)
    w_up_b = w_up.astype(BF16)
    w_down_b = w_down.astype(BF16)

    tm = min(1024, T)
    for layer in range(depth):
        if layer % 2 == 0:
            e = layer // 2
            proj = _norm_matmul(xf, ln_mix[layer], w_in_even_b, layer=e, n_cols=w_in_even.shape[2], tm=tm, tn=1024)
            o_a = _hgrn(proj, hgrn_lb, hgrn_norm[e], batch=batch, seq=seq, n_heads=hg_heads, col0=0, slot=e)
            o_b = _moba(proj, q_norm[e], k_norm[e], batch=batch, seq=seq, n_heads=mb_heads, col0=4 * hg_width)
            xf = _proj_residual([o_a, o_b], w_out_even_b, xf, layer=e, tm=tm, tn=512)
        else:
            o = layer // 2
            w_dt = jnp.pad(w_in_ssm[o][:, ssm_main:], ((0, 0), (0, V7X_LANES - ssm_heads))).astype(BF16)
            zxbc, dt_raw = _norm_matmul(xf, ln_mix[layer], w_in_ssm_b, w_dt, layer=o, n_cols=ssm_main, tm=tm, tn=1024)
            gy = _ssd(zxbc, dt_raw, conv_w, conv_b, dt_bias, a_log, d_skip, ssm_norm, layer=o, batch=batch, seq=seq)
            xf = _proj_residual([gy], w_out_ssm_b, xf, layer=o, tm=tm, tn=512)
        xf = _ffn(xf, ln_ffn[layer], w_gate_b, w_up_b, w_down_b, layer=layer, tm=tm, tf=512)
    return xf.reshape(batch, seq, d_model)
```

```python
import functools
import math

import jax
import jax.numpy as jnp
import numpy as np
from jax import lax
from jax.experimental import pallas as pl
from jax.experimental.pallas import tpu as pltpu

F32 = jnp.float32
BF16 = jnp.bfloat16
EPS = 1e-6
NEG_INF = float("-inf")
LOG2_E = 1.4426950408889634

HG_DIM = 128
MB_DIM = 128
MB_BLOCK = 256
MB_TOPK = 3
SSM_HEADDIM = 64
SSM_STATE = 128
SSM_GROUPS = 8
SSM_CONV = 4

V7X_LANES = 128
V7X_SUBLANES = 8
V7X_VMEM_BYTES = 64 * 1024 * 1024
V7X_MXU_COLS = 256

MATMUL_ROW_TILE = 1024
WEIGHT_TILE_BYTES = 8 * 1024 * 1024
FFN_COL_TILE = 512

HGRN_CHUNK = 256
HGRN_HEADS_PER_STEP = 8
SSD_GROUPS_PER_STEP = 8
MOBA_HEADS_PER_STEP = 2
SSD_CHUNK = 256
NORM_ROWS = 128
FFN_OUT_COLS = 512


def _cparams(semantics, vmem_bytes):
    return pltpu.CompilerParams(dimension_semantics=semantics, vmem_limit_bytes=int(vmem_bytes))


def _vmem_limit(estimate_bytes):
    return min(V7X_VMEM_BYTES - 6 * 1024 * 1024, max(32 * 1024 * 1024, int(estimate_bytes * 1.25)))


def _col_tile(k_rows, n_cols):
    tiles = [t for t in range(V7X_MXU_COLS, n_cols + 1, V7X_MXU_COLS)
             if n_cols % t == 0 and k_rows * t * 2 <= WEIGHT_TILE_BYTES]
    assert tiles, (k_rows, n_cols)
    return tiles[-1]


def _silu(x):
    h = 0.5 * x
    return h + h * jnp.tanh(h)


def _rmsnorm_rows(x_ref, gain_ref, h_ref):
    n_steps = x_ref.shape[0] // NORM_ROWS

    def body(r, carry):
        rows = pl.ds(pl.multiple_of(r * NORM_ROWS, NORM_ROWS), NORM_ROWS)
        x = x_ref[rows, :]
        ms = jnp.mean(x * x, axis=-1, keepdims=True)
        h_ref[rows, :] = (x * lax.rsqrt(ms + EPS) * gain_ref[...]).astype(h_ref.dtype)
        return carry

    lax.fori_loop(0, n_steps, body, 0)


def _norm_matmul_kernel(x_ref, gain_ref, w_ref, o_ref, h_ref):
    @pl.when(pl.program_id(1) == 0)
    def _():
        _rmsnorm_rows(x_ref, gain_ref, h_ref)

    o_ref[...] = jnp.dot(h_ref[...], w_ref[...], preferred_element_type=F32).astype(o_ref.dtype)


def _norm_matmul_extra_kernel(x_ref, gain_ref, w_ref, w2_ref, o_ref, o2_ref, h_ref):
    @pl.when(pl.program_id(1) == 0)
    def _():
        _rmsnorm_rows(x_ref, gain_ref, h_ref)
        o2_ref[...] = jnp.dot(h_ref[...], w2_ref[...], preferred_element_type=F32)

    o_ref[...] = jnp.dot(h_ref[...], w_ref[...], preferred_element_type=F32).astype(o_ref.dtype)


def _norm_matmul(x, gain, w, w_extra=None, *, layer, n_cols, tm, tn):
    T, D = x.shape
    N = n_cols
    grid = (T // tm, N // tn)
    in_specs = [
        pl.BlockSpec((tm, D), lambda i, j: (i, 0)),
        pl.BlockSpec((1, D), lambda i, j: (0, 0)),
        pl.BlockSpec((None, D, tn), lambda i, j: (layer, 0, j)),
    ]
    out_shape = [jax.ShapeDtypeStruct((T, N), F32)]
    out_specs = [pl.BlockSpec((tm, tn), lambda i, j: (i, j))]
    args = [x, gain.reshape(1, D), w]
    est = 2 * tm * D * 4 + tm * D * 2 + 2 * D * tn * 2 + 3 * tm * tn * 4
    if w_extra is None:
        body = _norm_matmul_kernel
    else:
        body = _norm_matmul_extra_kernel
        n2 = w_extra.shape[1]
        in_specs.append(pl.BlockSpec((D, n2), lambda i, j: (0, 0)))
        out_shape.append(jax.ShapeDtypeStruct((T, n2), F32))
        out_specs.append(pl.BlockSpec((tm, n2), lambda i, j: (i, 0)))
        args.append(w_extra)
        est += 2 * D * n2 * 2 + 2 * tm * n2 * 4
    res = pl.pallas_call(
        body,
        grid=grid,
        in_specs=in_specs,
        out_specs=out_specs,
        out_shape=out_shape,
        scratch_shapes=[pltpu.VMEM((tm, D), BF16)],
        compiler_params=_cparams(("parallel", "arbitrary"), _vmem_limit(est)),
        name="norm_in_proj",
    )(*args)
    return res[0] if w_extra is None else res


def _proj_residual_kernel(*refs, n_lhs):
    lhs_refs = refs[:n_lhs]
    w_ref, x_ref, o_ref = refs[n_lhs:]
    acc = x_ref[...]
    k0 = 0
    for a_ref in lhs_refs:
        k1 = k0 + a_ref.shape[1]
        acc = acc + jnp.dot(a_ref[...], w_ref[k0:k1, :], preferred_element_type=F32)
        k0 = k1
    o_ref[...] = acc


def _proj_residual(lhs_list, w, x, *, layer, tm, tn):
    T, N = x.shape
    K = w.shape[1]
    grid = (T // tm, N // tn)
    in_specs = [pl.BlockSpec((tm, a.shape[1]), lambda i, j: (i, 0)) for a in lhs_list]
    in_specs += [
        pl.BlockSpec((None, K, tn), lambda i, j: (layer, 0, j)),
        pl.BlockSpec((tm, tn), lambda i, j: (i, j)),
    ]
    est = 2 * tm * K * 2 + 2 * K * tn * 2 + 5 * tm * tn * 4
    return pl.pallas_call(
        functools.partial(_proj_residual_kernel, n_lhs=len(lhs_list)),
        grid=grid,
        in_specs=in_specs,
        out_specs=pl.BlockSpec((tm, tn), lambda i, j: (i, j)),
        out_shape=jax.ShapeDtypeStruct((T, N), F32),
        compiler_params=_cparams(("parallel", "arbitrary"), _vmem_limit(est)),
        name="out_proj_residual",
    )(*lhs_list, w, x)


def _ffn_kernel(x_ref, gain_ref, wg_ref, wu_ref, wd_ref, o_ref, h_ref):
    f = pl.program_id(1)

    @pl.when(f == 0)
    def _():
        _rmsnorm_rows(x_ref, gain_ref, h_ref)

    h = h_ref[...]
    g = jnp.dot(h, wg_ref[...], preferred_element_type=F32)
    u = jnp.dot(h, wu_ref[...], preferred_element_type=F32)
    a = (_silu(g) * u).astype(BF16)
    n_out = o_ref.shape[1]

    def down(c0):
        cols = slice(c0, min(c0 + FFN_OUT_COLS, n_out))
        return cols, jnp.dot(a, wd_ref[:, cols], preferred_element_type=F32)

    @pl.when(f == 0)
    def _():
        for c0 in range(0, n_out, FFN_OUT_COLS):
            cols, d = down(c0)
            o_ref[:, cols] = x_ref[:, cols] + d

    @pl.when(f != 0)
    def _():
        for c0 in range(0, n_out, FFN_OUT_COLS):
            cols, d = down(c0)
            o_ref[:, cols] += d


def _ffn(x, gain, wg, wu, wd, *, layer, tm, tf):
    T, D = x.shape
    F = wg.shape[2]
    grid = (T // tm, F // tf)
    est = 4 * tm * D * 4 + tm * D * 2 + 3 * 2 * D * tf * 2 + 3 * tm * tf * 4 + tm * FFN_OUT_COLS * 4
    return pl.pallas_call(
        _ffn_kernel,
        grid=grid,
        in_specs=[
            pl.BlockSpec((tm, D), lambda i, f: (i, 0)),
            pl.BlockSpec((1, D), lambda i, f: (0, 0)),
            pl.BlockSpec((None, D, tf), lambda i, f: (layer, 0, f)),
            pl.BlockSpec((None, D, tf), lambda i, f: (layer, 0, f)),
            pl.BlockSpec((None, tf, D), lambda i, f: (layer, f, 0)),
        ],
        out_specs=pl.BlockSpec((tm, D), lambda i, f: (i, 0)),
        out_shape=jax.ShapeDtypeStruct((T, D), F32),
        scratch_shapes=[pltpu.VMEM((tm, D), BF16)],
        compiler_params=_cparams(("parallel", "arbitrary"), _vmem_limit(est)),
        name="swiglu_ffn",
    )(x, gain.reshape(1, D), wg, wu, wd)


def _cumsum_rows(a):
    R, D = a.shape
    S8 = V7X_SUBLANES
    a3 = a.reshape(R // S8, S8, D)
    sub = lax.broadcasted_iota(jnp.int32, a3.shape, 1)
    shift = 1
    while shift < S8:
        a3 = a3 + jnp.where(sub >= shift, pltpu.roll(a3, shift, 1), 0.0)
        shift *= 2
    groups = []
    carry = None
    for g in range(R // S8):
        grp = a3[g]
        if carry is not None:
            grp = grp + carry
        groups.append(grp)
        carry = grp[S8 - 1:S8]
    return jnp.concatenate(groups, axis=0)


def _hgrn_level_table(n):
    t = np.arange(n)[:, None]
    s = np.arange(n)[None, :]
    lv = np.floor(np.log2(np.maximum(t ^ s, 1))).astype(np.int32)
    return jnp.asarray(np.where(t > s, lv, np.where(t == s, -1, -2)).astype(np.int32))


def _hgrn_kernel(q_ref, f_ref, i_ref, g_ref, lb_ref, nw_ref, lv_ref, o_ref, st_ref, *, slot):
    @pl.when(pl.program_id(2) == 0)
    def _():
        st_ref[...] = jnp.zeros_like(st_ref)

    for hh in range(q_ref.shape[1] // HG_DIM):
        cols = slice(hh * HG_DIM, (hh + 1) * HG_DIM)
        _hgrn_head(q_ref, f_ref, i_ref, g_ref, lb_ref, nw_ref, lv_ref, o_ref, st_ref.at[hh], cols, slot)


def _hgrn_head(q_ref, f_ref, i_ref, g_ref, lb_ref, nw_ref, lv_ref, o_ref, st_ref, cols, slot):
    C, D = q_ref.shape[0], HG_DIM
    H = lv_ref.shape[0]
    S8 = V7X_SUBLANES

    lbl = lb_ref[:, cols]
    ex = jnp.exp(lbl - jnp.max(lbl, axis=0, keepdims=True))
    lb = jnp.sum(ex[: slot + 1], axis=0, keepdims=True) / jnp.sum(ex, axis=0, keepdims=True)

    th = jnp.tanh(0.5 * f_ref[:, cols])
    q = _silu(q_ref[:, cols])
    log_f = jnp.log(lb + (1.0 - lb) * (0.5 + 0.5 * th))
    k = (1.0 - lb) * (0.5 - 0.5 * th)
    v = i_ref[:, cols]
    v_b = v.astype(BF16)

    b = _cumsum_rows(log_f) * LOG2_E
    row = lax.broadcasted_iota(jnp.int32, (C, D), 0)
    b8 = b.reshape(C // S8, S8, D)

    def bcast_row(r):
        return jnp.broadcast_to(b8[:, r:r + 1, :], b8.shape).reshape(C, D)

    lv = lv_ref[...]
    diag = jnp.sum(q * k, axis=1, keepdims=True)
    tiles = [jnp.where(lv == -1, diag[j * H:(j + 1) * H], 0.0) for j in range(C // H)]
    cross = None

    n_levels = int(math.log2(C))
    for lvl in range(n_levels):
        half = 1 << lvl
        blk = 2 * half
        is_q = (row & half) != 0
        if blk == 2:
            beta = jnp.where(is_q, pltpu.roll(b, 1, 0), b)
        elif blk < S8:
            beta = bcast_row(half - 1)
            for r0 in range(blk, S8, blk):
                beta = jnp.where((row & (S8 - 1)) >= r0, bcast_row(r0 + half - 1), beta)
        else:
            mid = b.reshape(C // blk, blk, D)[:, half - 1:half, :]
            beta = jnp.broadcast_to(mid, (C // blk, blk, D)).reshape(C, D)
        x = (jnp.where(is_q, q, k) * jnp.exp2(-jnp.abs(b - beta))).astype(BF16)
        if blk <= H:
            for j in range(C // H):
                xj = x[j * H:(j + 1) * H]
                p = lax.dot_general(xj, xj, (((1,), (1,)), ((), ())), preferred_element_type=F32)
                tiles[j] = jnp.where(lv == lvl, p, tiles[j])
        else:
            cross = lax.dot_general(x[H:], x[:H], (((1,), (1,)), ((), ())), preferred_element_type=F32)

    st = st_ref[...]
    o_top = jnp.dot(tiles[0].astype(BF16), v_b[:H], preferred_element_type=F32)
    o_bot = jnp.dot(jnp.concatenate([cross, tiles[1]], axis=1).astype(BF16), v_b, preferred_element_type=F32)
    qe = (q * jnp.exp2(b)).astype(BF16)
    o = jnp.concatenate([o_top, o_bot], axis=0) + lax.dot_general(
        qe, st.astype(BF16), (((1,), (1,)), ((), ())), preferred_element_type=F32)

    b_last = b[C - 1:C, :]
    k_dec = (k * jnp.exp2(b_last - b)).astype(BF16)
    v_t = v.T.astype(BF16)
    st_ref[...] = st * jnp.exp2(b_last) + jnp.dot(v_t, k_dec, preferred_element_type=F32)

    ms = jnp.mean(o * o, axis=-1, keepdims=True)
    o = o * lax.rsqrt(ms + EPS) * nw_ref[...] * _silu(g_ref[:, cols])
    o_ref[:, cols] = o.astype(o_ref.dtype)


def _hgrn(proj, lb_table, norm_w, *, batch, seq, n_heads, col0, slot):
    T = proj.shape[0]
    C = HGRN_CHUNK
    H = C // 2
    nc = seq // C
    hps = HGRN_HEADS_PER_STEP
    wb = hps * HG_DIM
    assert n_heads % hps == 0 and col0 % wb == 0 and (n_heads * HG_DIM) % wb == 0
    n_slots = lb_table.shape[0]

    def sec(s):
        cb = (col0 + s * n_heads * HG_DIM) // wb
        return pl.BlockSpec((C, wb), lambda b, h, c: (b * nc + c, cb + h))

    return pl.pallas_call(
        functools.partial(_hgrn_kernel, slot=slot),
        grid=(batch, n_heads // hps, nc),
        in_specs=[
            sec(0), sec(1), sec(2), sec(3),
            pl.BlockSpec((n_slots, wb), lambda b, h, c: (0, h)),
            pl.BlockSpec((1, HG_DIM), lambda b, h, c: (0, 0)),
            pl.BlockSpec((H, H), lambda b, h, c: (0, 0)),
        ],
        out_specs=pl.BlockSpec((C, wb), lambda b, h, c: (b * nc + c, h)),
        out_shape=jax.ShapeDtypeStruct((T, n_heads * HG_DIM), BF16),
        scratch_shapes=[pltpu.VMEM((hps, HG_DIM, HG_DIM), F32)],
        compiler_params=_cparams(("parallel", "parallel", "arbitrary"), 32 * 1024 * 1024),
        name="hgrn2_scan",
    )(proj, proj, proj, proj, lb_table, norm_w.reshape(1, HG_DIM), _hgrn_level_table(H))


def _moba_kernel(q_ref, k_ref, v_ref, qw_ref, kw_ref, o_ref):
    for hh in range(q_ref.shape[1] // MB_DIM):
        cols = slice(hh * MB_DIM, (hh + 1) * MB_DIM)
        _moba_head(q_ref.at[:, cols], k_ref.at[:, cols], v_ref.at[:, cols], qw_ref, kw_ref, o_ref.at[:, cols])


def _moba_head(q_ref, k_ref, v_ref, qw_ref, kw_ref, o_ref):
    S, D = q_ref.shape
    BLK = MB_BLOCK
    nb = S // BLK
    scale = D ** -0.5

    q = q_ref[...]
    k = k_ref[...]
    qn = q * lax.rsqrt(jnp.mean(q * q, axis=-1, keepdims=True) + EPS) * qw_ref[...]
    kn = k * lax.rsqrt(jnp.mean(k * k, axis=-1, keepdims=True) + EPS) * kw_ref[...]
    k_mean = jnp.mean(kn.reshape(nb, BLK, D), axis=1)
    qn_t = qn.T
    gate_t = jnp.dot(k_mean, qn_t, precision=lax.Precision.HIGHEST,
                     preferred_element_type=F32)
    qs_t = (qn_t * (scale * LOG2_E)).astype(BF16)
    kn_b = kn.astype(BF16)
    v_t = v_ref[...].T.astype(BF16)

    kk = lax.broadcasted_iota(jnp.int32, (BLK, BLK), 0)
    qq = lax.broadcasted_iota(jnp.int32, (BLK, BLK), 1)
    causal = kk <= qq

    for j in range(nb):
        cols = slice(j * BLK, (j + 1) * BLK)
        q_j = qs_t[:, cols]
        n_keep = min(MB_TOPK, j)
        selected = None
        if j > n_keep:
            g = [gate_t[m:m + 1, cols] for m in range(j)]
            selected = []
            for n in range(j):
                beaten = jnp.zeros((1, BLK), jnp.int32)
                for m in range(j):
                    if m < n:
                        beaten = beaten + (g[m] >= g[n]).astype(jnp.int32)
                    elif m > n:
                        beaten = beaten + (g[m] > g[n]).astype(jnp.int32)
                selected.append(beaten < n_keep)
        scores = []
        for n in range(j + 1):
            s = jnp.dot(kn_b[n * BLK:(n + 1) * BLK, :], q_j, preferred_element_type=F32)
            if n == j:
                s = jnp.where(causal, s, NEG_INF)
            elif selected is not None:
                s = jnp.where(selected[n], s, NEG_INF)
            scores.append(s)
        m_run = jnp.max(scores[0], axis=0, keepdims=True)
        for s in scores[1:]:
            m_run = jnp.maximum(m_run, jnp.max(s, axis=0, keepdims=True))
        denom = jnp.zeros((1, BLK), F32)
        acc = jnp.zeros((D, BLK), F32)
        for n, s in enumerate(scores):
            p = jnp.exp2(s - m_run)
            denom = denom + jnp.sum(p, axis=0, keepdims=True)
            acc = acc + jnp.dot(v_t[:, n * BLK:(n + 1) * BLK], p.astype(BF16), preferred_element_type=F32)
        o_ref[j * BLK:(j + 1) * BLK, :] = (acc / denom).T.astype(o_ref.dtype)


def _moba(proj, q_norm_w, k_norm_w, *, batch, seq, n_heads, col0):
    T = proj.shape[0]
    hps = MOBA_HEADS_PER_STEP
    wb = hps * MB_DIM
    assert n_heads % hps == 0 and col0 % wb == 0 and (n_heads * MB_DIM) % wb == 0

    def sec(s):
        cb = (col0 + s * n_heads * MB_DIM) // wb
        return pl.BlockSpec((seq, wb), lambda b, h: (b, cb + h))

    return pl.pallas_call(
        _moba_kernel,
        grid=(batch, n_heads // hps),
        in_specs=[
            sec(0), sec(1), sec(2),
            pl.BlockSpec((1, MB_DIM), lambda b, h: (0, 0)),
            pl.BlockSpec((1, MB_DIM), lambda b, h: (0, 0)),
        ],
        out_specs=pl.BlockSpec((seq, wb), lambda b, h: (b, h)),
        out_shape=jax.ShapeDtypeStruct((T, n_heads * MB_DIM), BF16),
        compiler_params=_cparams(("parallel", "parallel"), 48 * 1024 * 1024),
        name="moba_attention",
    )(proj, proj, proj, q_norm_w.reshape(1, MB_DIM), k_norm_w.reshape(1, MB_DIM))


def _causal_conv_silu(raw, tail, w_ref, bias_ref):
    K = SSM_CONV
    acc = raw * w_ref[K - 1:K, :] + bias_ref[...]
    row8 = lax.broadcasted_iota(jnp.int32, tail.shape, 0)
    for d in range(1, K):
        sh = pltpu.roll(raw, d, 0)
        head = jnp.where(row8 < d, pltpu.roll(tail, d, 0), sh[:V7X_SUBLANES])
        sh = jnp.concatenate([head, sh[V7X_SUBLANES:]], axis=0)
        acc = acc + sh * w_ref[K - 1 - d:K - d, :]
    return _silu(acc)


def _expand_heads(cols, n_heads, width, lane0=0):
    rows = cols.shape[0]
    per_vreg = V7X_LANES // width
    lane = lax.broadcasted_iota(jnp.int32, (rows, V7X_LANES), 1)
    blocks = []
    for h0 in range(lane0, lane0 + n_heads, per_vreg):
        blk = jnp.broadcast_to(cols[:, h0:h0 + 1], (rows, V7X_LANES))
        for k in range(1, per_vreg):
            nxt = jnp.broadcast_to(cols[:, h0 + k:h0 + k + 1], (rows, V7X_LANES))
            blk = jnp.where(lane >= k * width, nxt, blk)
        blocks.append(blk)
    return jnp.concatenate(blocks, axis=1)


def _ssd_kernel(z_ref, x_ref, bm_ref, cm_ref, dt_ref, cwx_ref, cwb_ref, cwc_ref, cbx_ref, cbb_ref, cbc_ref,
                dtb_ref, alog_ref, dsk_ref, nw_ref, o_ref, st_ref, tail_ref, cs2_ref, ecs_ref, wd_ref, key_ref,
                *, group_width):
    L = x_ref.shape[0]
    W, N = group_width, SSM_STATE
    gps = x_ref.shape[1] // W
    all_groups = gps == st_ref.shape[0]

    @pl.when(pl.program_id(2) == 0)
    def _():
        pre = dt_ref[...] + dtb_ref[...]
        dt_all = jnp.maximum(pre, 0.0) + jnp.log1p(jnp.exp(-jnp.abs(pre)))
        a_all = dt_all * (-jnp.exp(alog_ref[...]))
        cs_all = _cumsum_rows(a_all)
        cs2_all = cs_all * LOG2_E
        cs2_ref[...] = cs2_all
        ecs_ref[...] = jnp.exp(cs_all)
        wd_ref[...] = jnp.exp(cs_all[L - 1:L, :] - cs_all) * dt_all
        key_ref[...] = (cs2_all - jnp.log2(dt_all)).T

    for gg in range(gps):
        cw = slice(gg * W, (gg + 1) * W)
        cn = slice(gg * N, (gg + 1) * N)
        _ssd_group(z_ref.at[:, cw], x_ref.at[:, cw], bm_ref.at[:, cn], cm_ref.at[:, cn],
                   cwx_ref.at[:, cw], cwb_ref.at[:, cn], cwc_ref.at[:, cn],
                   cbx_ref.at[:, cw], cbb_ref.at[:, cn], cbc_ref.at[:, cn],
                   dsk_ref, nw_ref.at[:, cw], o_ref.at[:, cw], st_ref, tail_ref, cs2_ref, ecs_ref, wd_ref, key_ref,
                   gg if all_groups else pl.program_id(2) * gps + gg)


def _ssd_group(z_ref, x_ref, bm_ref, cm_ref, cwx_ref, cwb_ref, cwc_ref, cbx_ref, cbb_ref, cbc_ref,
               dsk_ref, nw_ref, o_ref, st_ref, tail_ref, cs2_ref, ecs_ref, wd_ref, key_ref, g):
    L, W = x_ref.shape
    P = SSM_HEADDIM
    hpg = W // P
    N = bm_ref.shape[1]

    @pl.when(pl.program_id(1) == 0)
    def _():
        st_ref[g] = jnp.zeros(st_ref.shape[1:], F32)
        tail_ref[g] = jnp.zeros(tail_ref.shape[1:], F32)

    x_raw = x_ref[...]
    b_raw = bm_ref[...]
    c_raw = cm_ref[...]
    tail = tail_ref[g]
    xs = _causal_conv_silu(x_raw, tail[:, :W], cwx_ref, cbx_ref)
    bm = _causal_conv_silu(b_raw, tail[:, W:W + N], cwb_ref, cbb_ref)
    cm = _causal_conv_silu(c_raw, tail[:, W + N:], cwc_ref, cbc_ref)
    tail_ref[g] = jnp.concatenate(
        [x_raw[L - V7X_SUBLANES:], b_raw[L - V7X_SUBLANES:], c_raw[L - V7X_SUBLANES:]], axis=1)

    if isinstance(g, int):
        head0 = lane0 = g * hpg
        cs2, ecs, wd, dsk = cs2_ref[...], ecs_ref[...], wd_ref[...], dsk_ref[...]
    else:
        head0, lane0 = pl.multiple_of(g * hpg, hpg), 0
        shift = (V7X_LANES - g * hpg) % V7X_LANES
        cs2, ecs, wd, dsk = (pltpu.roll(ref[...], shift, 1) for ref in (cs2_ref, ecs_ref, wd_ref, dsk_ref))
    key_t = key_ref[pl.ds(head0, hpg), :]

    xs_b = xs.astype(BF16)
    bm_b = bm.astype(BF16)
    cm_b = cm.astype(BF16)
    cb = lax.dot_general(cm_b, bm_b, (((1,), (1,)), ((), ())), preferred_element_type=F32)
    TL = V7X_LANES
    n_tiles = L // TL
    lower = (lax.broadcasted_iota(jnp.int32, (TL, TL), 0) >= lax.broadcasted_iota(jnp.int32, (TL, TL), 1))

    def head_rows(r, i):
        t0, s1 = i * TL, (i + 1) * TL
        seg = cs2[t0:s1, lane0 + r:lane0 + r + 1] - key_t[r:r + 1, :s1]
        diag = jnp.exp2(jnp.where(lower, seg[:, t0:], NEG_INF))
        lmat = diag if i == 0 else jnp.concatenate([jnp.exp2(seg[:, :t0]), diag], axis=1)
        return (cb[t0:s1, :s1] * lmat).astype(BF16)

    per_vreg = V7X_LANES // P
    lane_head = lax.broadcasted_iota(jnp.int32, (L, V7X_LANES), 1) // P
    zero = jnp.zeros((L, V7X_LANES), BF16)
    y_blocks = []
    for h0 in range(0, hpg, per_vreg):
        x_blk = xs_b[:, h0 * P:h0 * P + V7X_LANES]
        x_heads = [jnp.where(lane_head == k, x_blk, zero) for k in range(per_vreg)]
        y_rows = []
        for i in range(n_tiles):
            s1 = (i + 1) * TL
            lhs = jnp.concatenate([head_rows(h0 + k, i) for k in range(per_vreg)], axis=1)
            rhs = jnp.concatenate([xh[:s1] for xh in x_heads], axis=0)
            y_rows.append(jnp.dot(lhs, rhs, preferred_element_type=F32))
        y_blocks.append(jnp.concatenate(y_rows, axis=0))
    y = jnp.concatenate(y_blocks, axis=1)

    st = st_ref[g]
    y = y + jnp.dot(cm_b, st.astype(BF16), preferred_element_type=F32) * _expand_heads(ecs, hpg, P, lane0)
    y = y + _expand_heads(dsk, hpg, P, lane0) * xs

    xw = (xs * _expand_heads(wd, hpg, P, lane0)).astype(BF16)
    bm_t = bm.T.astype(BF16)
    st_ref[g] = (st * _expand_heads(ecs[L - 1:L, :], hpg, P, lane0)
                 + jnp.dot(bm_t, xw, preferred_element_type=F32))

    gy = y * _silu(z_ref[...])
    ms = jnp.mean(gy * gy, axis=-1, keepdims=True)
    o_ref[...] = (gy * lax.rsqrt(ms + EPS) * nw_ref[...]).astype(o_ref.dtype)


def _pad_lanes(a):
    return jnp.pad(a.reshape(1, -1), ((0, 0), (0, V7X_LANES - a.shape[0])))


def _ssd(zxbc, dt_raw, conv_w, conv_b, dt_bias, a_log, d_skip, norm_w, *, layer, batch, seq):
    T = zxbc.shape[0]
    G = SSM_GROUPS
    N = SSM_STATE
    heads = dt_bias.shape[1]
    assert heads <= V7X_LANES and dt_raw.shape[1] == V7X_LANES
    inner = heads * SSM_HEADDIM
    W = inner // G
    L = SSD_CHUNK
    nc = seq // L
    gps = SSD_GROUPS_PER_STEP
    assert G % gps == 0
    WB, NB = gps * W, gps * N
    xb, bb, cb_ = inner // WB, (2 * inner) // NB, (2 * inner + G * N) // NB
    wxb, wbb, wcb = 0, inner // NB, (inner + G * N) // NB

    dtb_p = _pad_lanes(dt_bias[layer])
    alog_p = _pad_lanes(a_log[layer])
    dsk_p = _pad_lanes(d_skip[layer])
    conv_b3 = conv_b.reshape(conv_b.shape[0], 1, -1)
    norm_w3 = norm_w.reshape(norm_w.shape[0], 1, -1)

    rows = lambda b, c, g: b * nc + c
    return pl.pallas_call(
        functools.partial(_ssd_kernel, group_width=W),
        grid=(batch, nc, G // gps),
        in_specs=[
            pl.BlockSpec((L, WB), lambda b, c, g: (rows(b, c, g), g)),
            pl.BlockSpec((L, WB), lambda b, c, g: (rows(b, c, g), xb + g)),
            pl.BlockSpec((L, NB), lambda b, c, g: (rows(b, c, g), bb + g)),
            pl.BlockSpec((L, NB), lambda b, c, g: (rows(b, c, g), cb_ + g)),
            pl.BlockSpec((L, V7X_LANES), lambda b, c, g: (rows(b, c, g), 0)),
            pl.BlockSpec((None, SSM_CONV, WB), lambda b, c, g: (layer, 0, wxb + g)),
            pl.BlockSpec((None, SSM_CONV, NB), lambda b, c, g: (layer, 0, wbb + g)),
            pl.BlockSpec((None, SSM_CONV, NB), lambda b, c, g: (layer, 0, wcb + g)),
            pl.BlockSpec((None, 1, WB), lambda b, c, g: (layer, 0, wxb + g)),
            pl.BlockSpec((None, 1, NB), lambda b, c, g: (layer, 0, wbb + g)),
            pl.BlockSpec((None, 1, NB), lambda b, c, g: (layer, 0, wcb + g)),
            pl.BlockSpec((1, V7X_LANES), lambda b, c, g: (0, 0)),
            pl.BlockSpec((1, V7X_LANES), lambda b, c, g: (0, 0)),
            pl.BlockSpec((1, V7X_LANES), lambda b, c, g: (0, 0)),
            pl.BlockSpec((None, 1, WB), lambda b, c, g: (layer, 0, g)),
        ],
        out_specs=pl.BlockSpec((L, WB), lambda b, c, g: (rows(b, c, g), g)),
        out_shape=jax.ShapeDtypeStruct((T, inner), BF16),
        scratch_shapes=[
            pltpu.VMEM((G, N, W), F32),
            pltpu.VMEM((G, V7X_SUBLANES, W + 2 * N), F32),
            pltpu.VMEM((L, V7X_LANES), F32),
            pltpu.VMEM((L, V7X_LANES), F32),
            pltpu.VMEM((L, V7X_LANES), F32),
            pltpu.VMEM((V7X_LANES, L), F32),
        ],
        compiler_params=_cparams(("parallel", "arbitrary", "arbitrary"), 48 * 1024 * 1024),
        name="conv_ssd_scan",
    )(zxbc, zxbc, zxbc, zxbc, dt_raw, conv_w, conv_w, conv_w, conv_b3, conv_b3, conv_b3,
      dtb_p, alog_p, dsk_p, norm_w3)


def kernel(x, ln_mix, ln_ffn, w_in_even, w_out_even, hgrn_lb, hgrn_norm, q_norm, k_norm, w_in_ssm, conv_w,
           conv_b, dt_bias, a_log, d_skip, ssm_norm, w_out_ssm, w_gate, w_up, w_down):
    batch, seq, d_model = x.shape
    depth = ln_mix.shape[0]
    T = batch * seq
    xf = x.reshape(T, d_model)

    hg_width = hgrn_lb.shape[1]
    hg_heads = hg_width // HG_DIM
    mb_width = (w_in_even.shape[2] - 4 * hg_width) // 3
    mb_heads = mb_width // MB_DIM
    ssm_heads = dt_bias.shape[1]
    ssm_main = w_in_ssm.shape[2] - ssm_heads

    w_in_even_b = w_in_even.astype(BF16)
    w_out_even_b = w_out_even.astype(BF16)
    w_in_ssm_b = w_in_ssm.astype(BF16)
    w_out_ssm_b = w_out_ssm.astype(BF16)
    w_gate_b = w_gate.astype(BF16)
    w_up_b = w_up.astype(BF16)
    w_down_b = w_down.astype(BF16)

    tm = min(MATMUL_ROW_TILE, T)
    for layer in range(depth):
        if layer % 2 == 0:
            e = layer // 2
            n_in = w_in_even.shape[2]
            proj = _norm_matmul(xf, ln_mix[layer], w_in_even_b, layer=e, n_cols=n_in, tm=tm,
                                tn=_col_tile(d_model, n_in))
            o_a = _hgrn(proj, hgrn_lb, hgrn_norm[e], batch=batch, seq=seq, n_heads=hg_heads, col0=0, slot=e)
            o_b = _moba(proj, q_norm[e], k_norm[e], batch=batch, seq=seq, n_heads=mb_heads, col0=4 * hg_width)
            xf = _proj_residual([o_a, o_b], w_out_even_b, xf, layer=e, tm=tm,
                                tn=_col_tile(w_out_even.shape[1], d_model))
        else:
            o = layer // 2
            w_dt = jnp.pad(w_in_ssm[o][:, ssm_main:], ((0, 0), (0, V7X_LANES - ssm_heads))).astype(BF16)
            zxbc, dt_raw = _norm_matmul(xf, ln_mix[layer], w_in_ssm_b, w_dt, layer=o, n_cols=ssm_main, tm=tm,
                                        tn=_col_tile(d_model, ssm_main))
            gy = _ssd(zxbc, dt_raw, conv_w, conv_b, dt_bias, a_log, d_skip, ssm_norm, layer=o, batch=batch, seq=seq)
            xf = _proj_residual([gy], w_out_ssm_b, xf, layer=o, tm=tm, tn=_col_tile(w_out_ssm.shape[1], d_model))
        xf = _ffn(xf, ln_ffn[layer], w_gate_b, w_up_b, w_down_b, layer=layer, tm=tm, tf=FFN_COL_TILE)
    return xf.reshape(batch, seq, d_model)
```

```python
import functools
import math

import jax
import jax.numpy as jnp
import numpy as np
from jax import lax
from jax.experimental import pallas as pl
from jax.experimental.pallas import tpu as pltpu

F32 = jnp.float32
BF16 = jnp.bfloat16
EPS = 1e-6
NEG_INF = float("-inf")
LOG2_E = 1.4426950408889634

HG_DIM = 128
MB_DIM = 128
MB_BLOCK = 256
MB_TOPK = 3
SSM_HEADDIM = 64
SSM_STATE = 128
SSM_GROUPS = 8
SSM_CONV = 4

V7X_LANES = 128
V7X_SUBLANES = 8
V7X_VMEM_BYTES = 64 * 1024 * 1024
V7X_MXU_COLS = 256
BF16_TILE_ROWS = 16

MATMUL_ROW_TILE = 1024
WEIGHT_TILE_BYTES = 8 * 1024 * 1024
FFN_COL_TILE = 512

HGRN_CHUNK = 256
HGRN_HEADS_PER_STEP = 8
SSD_GROUPS_PER_STEP = 8
MOBA_HEADS_PER_STEP = 2
SSD_CHUNK = 256
NORM_ROWS = 128
FFN_OUT_COLS = 512


def _cparams(semantics, vmem_bytes):
    return pltpu.CompilerParams(dimension_semantics=semantics, vmem_limit_bytes=int(vmem_bytes))


def _vmem_limit(estimate_bytes):
    return min(V7X_VMEM_BYTES - 6 * 1024 * 1024, max(32 * 1024 * 1024, int(estimate_bytes * 1.25)))


def _col_tile(k_rows, n_cols):
    tiles = [t for t in range(V7X_MXU_COLS, n_cols + 1, V7X_MXU_COLS)
             if n_cols % t == 0 and k_rows * t * 2 <= WEIGHT_TILE_BYTES]
    assert tiles, (k_rows, n_cols)
    return tiles[-1]


def _side_casts(weights, grid):
    n_steps = math.prod(grid)

    def step(*ids):
        s = 0
        for k, g in zip(ids, grid):
            s = s * g + k
        return s

    in_specs, out_specs, out_shapes, args = [], [], [], []
    for arr, layer in weights:
        _, n_rows, n_cols = arr.shape
        n = max(c for c in range(1, n_steps + 1) if n_rows % c == 0 and (n_rows // c) % BF16_TILE_ROWS == 0)
        rows = n_rows // n
        in_specs.append(pl.BlockSpec(
            (None, rows, n_cols), lambda *ids, layer=layer, n=n: (layer, jnp.minimum(step(*ids), n - 1), 0)))
        out_specs.append(pl.BlockSpec((rows, n_cols), lambda *ids, n=n: (jnp.minimum(step(*ids), n - 1), 0)))
        out_shapes.append(jax.ShapeDtypeStruct((n_rows, n_cols), BF16))
        args.append(arr)
    return in_specs, out_specs, out_shapes, args


def _with_side_casts(body, n_in, n_out, n_side):
    if n_side == 0:
        return body

    def wrapped(*refs):
        side_in = refs[n_in:n_in + n_side]
        o0 = n_in + n_side
        side_out = refs[o0 + n_out:o0 + n_out + n_side]
        for src, dst in zip(side_in, side_out):
            dst[...] = src[...].astype(dst.dtype)
        body(*refs[:n_in], *refs[o0:o0 + n_out], *refs[o0 + n_out + n_side:])

    return wrapped


def _silu(x):
    h = 0.5 * x
    return h + h * jnp.tanh(h)


def _rmsnorm_rows(x_ref, gain_ref, h_ref):
    n_steps = x_ref.shape[0] // NORM_ROWS

    def body(r, carry):
        rows = pl.ds(pl.multiple_of(r * NORM_ROWS, NORM_ROWS), NORM_ROWS)
        x = x_ref[rows, :]
        ms = jnp.mean(x * x, axis=-1, keepdims=True)
        h_ref[rows, :] = (x * lax.rsqrt(ms + EPS) * gain_ref[...]).astype(h_ref.dtype)
        return carry

    lax.fori_loop(0, n_steps, body, 0)


def _norm_matmul_kernel(x_ref, gain_ref, w_ref, o_ref, h_ref):
    @pl.when(pl.program_id(1) == 0)
    def _():
        _rmsnorm_rows(x_ref, gain_ref, h_ref)

    o_ref[...] = jnp.dot(h_ref[...], w_ref[...], preferred_element_type=F32).astype(o_ref.dtype)


def _norm_matmul_extra_kernel(x_ref, gain_ref, w_ref, w2_ref, o_ref, o2_ref, h_ref):
    @pl.when(pl.program_id(1) == 0)
    def _():
        _rmsnorm_rows(x_ref, gain_ref, h_ref)
        o2_ref[...] = jnp.dot(h_ref[...], w2_ref[...], preferred_element_type=F32)

    o_ref[...] = jnp.dot(h_ref[...], w_ref[...], preferred_element_type=F32).astype(o_ref.dtype)


def _norm_matmul(x, gain, w, w_extra=None, *, n_cols, tm, tn):
    T, D = x.shape
    N = n_cols
    grid = (T // tm, N // tn)
    in_specs = [
        pl.BlockSpec((tm, D), lambda i, j: (i, 0)),
        pl.BlockSpec((1, D), lambda i, j: (0, 0)),
        pl.BlockSpec((D, tn), lambda i, j: (0, j)),
    ]
    out_shape = [jax.ShapeDtypeStruct((T, N), F32)]
    out_specs = [pl.BlockSpec((tm, tn), lambda i, j: (i, j))]
    args = [x, gain.reshape(1, D), w]
    est = 2 * tm * D * 4 + tm * D * 2 + 2 * D * tn * 2 + 3 * tm * tn * 4
    if w_extra is None:
        body = _norm_matmul_kernel
    else:
        body = _norm_matmul_extra_kernel
        n2 = w_extra.shape[1]
        in_specs.append(pl.BlockSpec((D, n2), lambda i, j: (0, 0)))
        out_shape.append(jax.ShapeDtypeStruct((T, n2), F32))
        out_specs.append(pl.BlockSpec((tm, n2), lambda i, j: (i, 0)))
        args.append(w_extra)
        est += 2 * D * n2 * 2 + 2 * tm * n2 * 4
    res = pl.pallas_call(
        body,
        grid=grid,
        in_specs=in_specs,
        out_specs=out_specs,
        out_shape=out_shape,
        scratch_shapes=[pltpu.VMEM((tm, D), BF16)],
        compiler_params=_cparams(("parallel", "arbitrary"), _vmem_limit(est)),
        name="norm_in_proj",
    )(*args)
    return res[0] if w_extra is None else res


def _proj_residual_kernel(*refs, n_lhs):
    lhs_refs = refs[:n_lhs]
    w_ref, x_ref, o_ref = refs[n_lhs:]
    acc = x_ref[...]
    k0 = 0
    for a_ref in lhs_refs:
        k1 = k0 + a_ref.shape[1]
        acc = acc + jnp.dot(a_ref[...], w_ref[k0:k1, :], preferred_element_type=F32)
        k0 = k1
    o_ref[...] = acc


def _proj_residual(lhs_list, w, x, *, tm, tn):
    T, N = x.shape
    K = w.shape[0]
    grid = (T // tm, N // tn)
    in_specs = [pl.BlockSpec((tm, a.shape[1]), lambda i, j: (i, 0)) for a in lhs_list]
    in_specs += [
        pl.BlockSpec((K, tn), lambda i, j: (0, j)),
        pl.BlockSpec((tm, tn), lambda i, j: (i, j)),
    ]
    est = 2 * tm * K * 2 + 2 * K * tn * 2 + 5 * tm * tn * 4
    return pl.pallas_call(
        functools.partial(_proj_residual_kernel, n_lhs=len(lhs_list)),
        grid=grid,
        in_specs=in_specs,
        out_specs=pl.BlockSpec((tm, tn), lambda i, j: (i, j)),
        out_shape=jax.ShapeDtypeStruct((T, N), F32),
        compiler_params=_cparams(("parallel", "arbitrary"), _vmem_limit(est)),
        name="out_proj_residual",
    )(*lhs_list, w, x)


def _ffn_kernel(x_ref, gain_ref, wg_ref, wu_ref, wd_ref, o_ref, h_ref):
    f = pl.program_id(1)

    @pl.when(f == 0)
    def _():
        _rmsnorm_rows(x_ref, gain_ref, h_ref)

    h = h_ref[...]
    g = jnp.dot(h, wg_ref[...], preferred_element_type=F32)
    u = jnp.dot(h, wu_ref[...], preferred_element_type=F32)
    a = (_silu(g) * u).astype(BF16)
    n_out = o_ref.shape[1]

    def down(c0):
        cols = slice(c0, min(c0 + FFN_OUT_COLS, n_out))
        return cols, jnp.dot(a, wd_ref[:, cols], preferred_element_type=F32)

    @pl.when(f == 0)
    def _():
        for c0 in range(0, n_out, FFN_OUT_COLS):
            cols, d = down(c0)
            o_ref[:, cols] = x_ref[:, cols] + d

    @pl.when(f != 0)
    def _():
        for c0 in range(0, n_out, FFN_OUT_COLS):
            cols, d = down(c0)
            o_ref[:, cols] += d


def _ffn(x, gain, wg, wu, wd, *, tm, tf, side=()):
    T, D = x.shape
    F = wg.shape[1]
    grid = (T // tm, F // tf)
    s_in, s_out, s_shapes, s_args = _side_casts(side, grid)
    est = 4 * tm * D * 4 + tm * D * 2 + 3 * 2 * D * tf * 2 + 3 * tm * tf * 4 + tm * FFN_OUT_COLS * 4
    res = pl.pallas_call(
        _with_side_casts(_ffn_kernel, 5, 1, len(side)),
        grid=grid,
        in_specs=[
            pl.BlockSpec((tm, D), lambda i, f: (i, 0)),
            pl.BlockSpec((1, D), lambda i, f: (0, 0)),
            pl.BlockSpec((D, tf), lambda i, f: (0, f)),
            pl.BlockSpec((D, tf), lambda i, f: (0, f)),
            pl.BlockSpec((tf, D), lambda i, f: (f, 0)),
        ] + s_in,
        out_specs=[pl.BlockSpec((tm, D), lambda i, f: (i, 0))] + s_out,
        out_shape=[jax.ShapeDtypeStruct((T, D), F32)] + s_shapes,
        scratch_shapes=[pltpu.VMEM((tm, D), BF16)],
        compiler_params=_cparams(("arbitrary", "arbitrary"), _vmem_limit(est)),
        name="swiglu_ffn",
    )(x, gain.reshape(1, D), wg, wu, wd, *s_args)
    return res[0], res[1:]


def _cumsum_rows(a):
    R, D = a.shape
    S8 = V7X_SUBLANES
    a3 = a.reshape(R // S8, S8, D)
    sub = lax.broadcasted_iota(jnp.int32, a3.shape, 1)
    shift = 1
    while shift < S8:
        a3 = a3 + jnp.where(sub >= shift, pltpu.roll(a3, shift, 1), 0.0)
        shift *= 2
    groups = []
    carry = None
    for g in range(R // S8):
        grp = a3[g]
        if carry is not None:
            grp = grp + carry
        groups.append(grp)
        carry = grp[S8 - 1:S8]
    return jnp.concatenate(groups, axis=0)


def _hgrn_level_table(n):
    t = np.arange(n)[:, None]
    s = np.arange(n)[None, :]
    lv = np.floor(np.log2(np.maximum(t ^ s, 1))).astype(np.int32)
    return jnp.asarray(np.where(t > s, lv, np.where(t == s, -1, -2)).astype(np.int32))


def _hgrn_kernel(q_ref, f_ref, i_ref, g_ref, lb_ref, nw_ref, lv_ref, o_ref, st_ref, *, slot):
    @pl.when(pl.program_id(2) == 0)
    def _():
        st_ref[...] = jnp.zeros_like(st_ref)

    for hh in range(q_ref.shape[1] // HG_DIM):
        cols = slice(hh * HG_DIM, (hh + 1) * HG_DIM)
        _hgrn_head(q_ref, f_ref, i_ref, g_ref, lb_ref, nw_ref, lv_ref, o_ref, st_ref.at[hh], cols, slot)


def _hgrn_head(q_ref, f_ref, i_ref, g_ref, lb_ref, nw_ref, lv_ref, o_ref, st_ref, cols, slot):
    C, D = q_ref.shape[0], HG_DIM
    H = lv_ref.shape[0]
    S8 = V7X_SUBLANES

    lbl = lb_ref[:, cols]
    ex = jnp.exp(lbl - jnp.max(lbl, axis=0, keepdims=True))
    lb = jnp.sum(ex[: slot + 1], axis=0, keepdims=True) / jnp.sum(ex, axis=0, keepdims=True)

    th = jnp.tanh(0.5 * f_ref[:, cols])
    q = _silu(q_ref[:, cols])
    log_f = jnp.log(lb + (1.0 - lb) * (0.5 + 0.5 * th))
    k = (1.0 - lb) * (0.5 - 0.5 * th)
    v = i_ref[:, cols]
    v_b = v.astype(BF16)

    b = _cumsum_rows(log_f) * LOG2_E
    row = lax.broadcasted_iota(jnp.int32, (C, D), 0)
    b8 = b.reshape(C // S8, S8, D)

    def bcast_row(r):
        return jnp.broadcast_to(b8[:, r:r + 1, :], b8.shape).reshape(C, D)

    lv = lv_ref[...]
    diag = jnp.sum(q * k, axis=1, keepdims=True)
    tiles = [jnp.where(lv == -1, diag[j * H:(j + 1) * H], 0.0) for j in range(C // H)]
    cross = None

    n_levels = int(math.log2(C))
    for lvl in range(n_levels):
        half = 1 << lvl
        blk = 2 * half
        is_q = (row & half) != 0
        if blk == 2:
            beta = jnp.where(is_q, pltpu.roll(b, 1, 0), b)
        elif blk < S8:
            beta = bcast_row(half - 1)
            for r0 in range(blk, S8, blk):
                beta = jnp.where((row & (S8 - 1)) >= r0, bcast_row(r0 + half - 1), beta)
        else:
            mid = b.reshape(C // blk, blk, D)[:, half - 1:half, :]
            beta = jnp.broadcast_to(mid, (C // blk, blk, D)).reshape(C, D)
        x = (jnp.where(is_q, q, k) * jnp.exp2(-jnp.abs(b - beta))).astype(BF16)
        if blk <= H:
            for j in range(C // H):
                xj = x[j * H:(j + 1) * H]
                p = lax.dot_general(xj, xj, (((1,), (1,)), ((), ())), preferred_element_type=F32)
                tiles[j] = jnp.where(lv == lvl, p, tiles[j])
        else:
            cross = lax.dot_general(x[H:], x[:H], (((1,), (1,)), ((), ())), preferred_element_type=F32)

    st = st_ref[...]
    o_top = jnp.dot(tiles[0].astype(BF16), v_b[:H], preferred_element_type=F32)
    o_bot = jnp.dot(jnp.concatenate([cross, tiles[1]], axis=1).astype(BF16), v_b, preferred_element_type=F32)
    qe = (q * jnp.exp2(b)).astype(BF16)
    o = jnp.concatenate([o_top, o_bot], axis=0) + lax.dot_general(
        qe, st.astype(BF16), (((1,), (1,)), ((), ())), preferred_element_type=F32)

    b_last = b[C - 1:C, :]
    k_dec = (k * jnp.exp2(b_last - b)).astype(BF16)
    v_t = v.T.astype(BF16)
    st_ref[...] = st * jnp.exp2(b_last) + jnp.dot(v_t, k_dec, preferred_element_type=F32)

    ms = jnp.mean(o * o, axis=-1, keepdims=True)
    o = o * lax.rsqrt(ms + EPS) * nw_ref[...] * _silu(g_ref[:, cols])
    o_ref[:, cols] = o.astype(o_ref.dtype)


def _hgrn(proj, lb_table, norm_w, *, batch, seq, n_heads, col0, slot, side=()):
    T = proj.shape[0]
    C = HGRN_CHUNK
    H = C // 2
    nc = seq // C
    hps = HGRN_HEADS_PER_STEP
    wb = hps * HG_DIM
    assert n_heads % hps == 0 and col0 % wb == 0 and (n_heads * HG_DIM) % wb == 0
    n_slots = lb_table.shape[0]

    def sec(s):
        cb = (col0 + s * n_heads * HG_DIM) // wb
        return pl.BlockSpec((C, wb), lambda b, h, c: (b * nc + c, cb + h))

    grid = (batch, n_heads // hps, nc)
    s_in, s_out, s_shapes, s_args = _side_casts(side, grid)
    res = pl.pallas_call(
        _with_side_casts(functools.partial(_hgrn_kernel, slot=slot), 7, 1, len(side)),
        grid=grid,
        in_specs=[
            sec(0), sec(1), sec(2), sec(3),
            pl.BlockSpec((n_slots, wb), lambda b, h, c: (0, h)),
            pl.BlockSpec((1, HG_DIM), lambda b, h, c: (0, 0)),
            pl.BlockSpec((H, H), lambda b, h, c: (0, 0)),
        ] + s_in,
        out_specs=[pl.BlockSpec((C, wb), lambda b, h, c: (b * nc + c, h))] + s_out,
        out_shape=[jax.ShapeDtypeStruct((T, n_heads * HG_DIM), BF16)] + s_shapes,
        scratch_shapes=[pltpu.VMEM((hps, HG_DIM, HG_DIM), F32)],
        compiler_params=_cparams(("arbitrary", "arbitrary", "arbitrary"), 40 * 1024 * 1024),
        name="hgrn2_scan",
    )(proj, proj, proj, proj, lb_table, norm_w.reshape(1, HG_DIM), _hgrn_level_table(H), *s_args)
    return res[0], res[1:]


def _moba_kernel(q_ref, k_ref, v_ref, qw_ref, kw_ref, o_ref):
    for hh in range(q_ref.shape[1] // MB_DIM):
        cols = slice(hh * MB_DIM, (hh + 1) * MB_DIM)
        _moba_head(q_ref.at[:, cols], k_ref.at[:, cols], v_ref.at[:, cols], qw_ref, kw_ref, o_ref.at[:, cols])


def _moba_head(q_ref, k_ref, v_ref, qw_ref, kw_ref, o_ref):
    S, D = q_ref.shape
    BLK = MB_BLOCK
    nb = S // BLK
    scale = D ** -0.5

    q = q_ref[...]
    k = k_ref[...]
    qn = q * lax.rsqrt(jnp.mean(q * q, axis=-1, keepdims=True) + EPS) * qw_ref[...]
    kn = k * lax.rsqrt(jnp.mean(k * k, axis=-1, keepdims=True) + EPS) * kw_ref[...]
    k_mean = jnp.mean(kn.reshape(nb, BLK, D), axis=1)
    qn_t = qn.T
    gate_t = jnp.dot(k_mean, qn_t, precision=lax.Precision.HIGHEST,
                     preferred_element_type=F32)
    qs_t = (qn_t * (scale * LOG2_E)).astype(BF16)
    kn_b = kn.astype(BF16)
    v_t = v_ref[...].T.astype(BF16)

    kk = lax.broadcasted_iota(jnp.int32, (BLK, BLK), 0)
    qq = lax.broadcasted_iota(jnp.int32, (BLK, BLK), 1)
    causal = kk <= qq

    for j in range(nb):
        cols = slice(j * BLK, (j + 1) * BLK)
        q_j = qs_t[:, cols]
        n_keep = min(MB_TOPK, j)
        selected = None
        if j > n_keep:
            g = [gate_t[m:m + 1, cols] for m in range(j)]
            selected = []
            for n in range(j):
                beaten = jnp.zeros((1, BLK), jnp.int32)
                for m in range(j):
                    if m < n:
                        beaten = beaten + (g[m] >= g[n]).astype(jnp.int32)
                    elif m > n:
                        beaten = beaten + (g[m] > g[n]).astype(jnp.int32)
                selected.append(beaten < n_keep)
        scores = []
        for n in range(j + 1):
            s = jnp.dot(kn_b[n * BLK:(n + 1) * BLK, :], q_j, preferred_element_type=F32)
            if n == j:
                s = jnp.where(causal, s, NEG_INF)
            elif selected is not None:
                s = jnp.where(selected[n], s, NEG_INF)
            scores.append(s)
        m_run = jnp.max(scores[0], axis=0, keepdims=True)
        for s in scores[1:]:
            m_run = jnp.maximum(m_run, jnp.max(s, axis=0, keepdims=True))
        denom = jnp.zeros((1, BLK), F32)
        acc = jnp.zeros((D, BLK), F32)
        for n, s in enumerate(scores):
            p = jnp.exp2(s - m_run)
            denom = denom + jnp.sum(p, axis=0, keepdims=True)
            acc = acc + jnp.dot(v_t[:, n * BLK:(n + 1) * BLK], p.astype(BF16), preferred_element_type=F32)
        o_ref[j * BLK:(j + 1) * BLK, :] = (acc / denom).T.astype(o_ref.dtype)


def _moba(proj, q_norm_w, k_norm_w, *, batch, seq, n_heads, col0, side=()):
    T = proj.shape[0]
    hps = MOBA_HEADS_PER_STEP
    wb = hps * MB_DIM
    assert n_heads % hps == 0 and col0 % wb == 0 and (n_heads * MB_DIM) % wb == 0

    def sec(s):
        cb = (col0 + s * n_heads * MB_DIM) // wb
        return pl.BlockSpec((seq, wb), lambda b, h: (b, cb + h))

    grid = (batch, n_heads // hps)
    s_in, s_out, s_shapes, s_args = _side_casts(side, grid)
    res = pl.pallas_call(
        _with_side_casts(_moba_kernel, 5, 1, len(side)),
        grid=grid,
        in_specs=[
            sec(0), sec(1), sec(2),
            pl.BlockSpec((1, MB_DIM), lambda b, h: (0, 0)),
            pl.BlockSpec((1, MB_DIM), lambda b, h: (0, 0)),
        ] + s_in,
        out_specs=[pl.BlockSpec((seq, wb), lambda b, h: (b, h))] + s_out,
        out_shape=[jax.ShapeDtypeStruct((T, n_heads * MB_DIM), BF16)] + s_shapes,
        compiler_params=_cparams(("arbitrary", "arbitrary"), 48 * 1024 * 1024),
        name="moba_attention",
    )(proj, proj, proj, q_norm_w.reshape(1, MB_DIM), k_norm_w.reshape(1, MB_DIM), *s_args)
    return res[0], res[1:]


def _causal_conv_silu(raw, tail, w_ref, bias_ref):
    K = SSM_CONV
    acc = raw * w_ref[K - 1:K, :] + bias_ref[...]
    row8 = lax.broadcasted_iota(jnp.int32, tail.shape, 0)
    for d in range(1, K):
        sh = pltpu.roll(raw, d, 0)
        head = jnp.where(row8 < d, pltpu.roll(tail, d, 0), sh[:V7X_SUBLANES])
        sh = jnp.concatenate([head, sh[V7X_SUBLANES:]], axis=0)
        acc = acc + sh * w_ref[K - 1 - d:K - d, :]
    return _silu(acc)


def _expand_heads(cols, n_heads, width, lane0=0):
    rows = cols.shape[0]
    per_vreg = V7X_LANES // width
    lane = lax.broadcasted_iota(jnp.int32, (rows, V7X_LANES), 1)
    blocks = []
    for h0 in range(lane0, lane0 + n_heads, per_vreg):
        blk = jnp.broadcast_to(cols[:, h0:h0 + 1], (rows, V7X_LANES))
        for k in range(1, per_vreg):
            nxt = jnp.broadcast_to(cols[:, h0 + k:h0 + k + 1], (rows, V7X_LANES))
            blk = jnp.where(lane >= k * width, nxt, blk)
        blocks.append(blk)
    return jnp.concatenate(blocks, axis=1)


def _ssd_kernel(z_ref, x_ref, bm_ref, cm_ref, dt_ref, cwx_ref, cwb_ref, cwc_ref, cbx_ref, cbb_ref, cbc_ref,
                dtb_ref, alog_ref, dsk_ref, nw_ref, o_ref, st_ref, tail_ref, cs2_ref, ecs_ref, wd_ref, key_ref,
                *, group_width):
    L = x_ref.shape[0]
    W, N = group_width, SSM_STATE
    gps = x_ref.shape[1] // W
    all_groups = gps == st_ref.shape[0]

    @pl.when(pl.program_id(2) == 0)
    def _():
        pre = dt_ref[...] + dtb_ref[...]
        dt_all = jnp.maximum(pre, 0.0) + jnp.log1p(jnp.exp(-jnp.abs(pre)))
        a_all = dt_all * (-jnp.exp(alog_ref[...]))
        cs_all = _cumsum_rows(a_all)
        cs2_all = cs_all * LOG2_E
        cs2_ref[...] = cs2_all
        ecs_ref[...] = jnp.exp(cs_all)
        wd_ref[...] = jnp.exp(cs_all[L - 1:L, :] - cs_all) * dt_all
        key_ref[...] = (cs2_all - jnp.log2(dt_all)).T

    for gg in range(gps):
        cw = slice(gg * W, (gg + 1) * W)
        cn = slice(gg * N, (gg + 1) * N)
        _ssd_group(z_ref.at[:, cw], x_ref.at[:, cw], bm_ref.at[:, cn], cm_ref.at[:, cn],
                   cwx_ref.at[:, cw], cwb_ref.at[:, cn], cwc_ref.at[:, cn],
                   cbx_ref.at[:, cw], cbb_ref.at[:, cn], cbc_ref.at[:, cn],
                   dsk_ref, nw_ref.at[:, cw], o_ref.at[:, cw], st_ref, tail_ref, cs2_ref, ecs_ref, wd_ref, key_ref,
                   gg if all_groups else pl.program_id(2) * gps + gg)


def _ssd_group(z_ref, x_ref, bm_ref, cm_ref, cwx_ref, cwb_ref, cwc_ref, cbx_ref, cbb_ref, cbc_ref,
               dsk_ref, nw_ref, o_ref, st_ref, tail_ref, cs2_ref, ecs_ref, wd_ref, key_ref, g):
    L, W = x_ref.shape
    P = SSM_HEADDIM
    hpg = W // P
    N = bm_ref.shape[1]

    @pl.when(pl.program_id(1) == 0)
    def _():
        st_ref[g] = jnp.zeros(st_ref.shape[1:], F32)
        tail_ref[g] = jnp.zeros(tail_ref.shape[1:], F32)

    x_raw = x_ref[...]
    b_raw = bm_ref[...]
    c_raw = cm_ref[...]
    tail = tail_ref[g]
    xs = _causal_conv_silu(x_raw, tail[:, :W], cwx_ref, cbx_ref)
    bm = _causal_conv_silu(b_raw, tail[:, W:W + N], cwb_ref, cbb_ref)
    cm = _causal_conv_silu(c_raw, tail[:, W + N:], cwc_ref, cbc_ref)
    tail_ref[g] = jnp.concatenate(
        [x_raw[L - V7X_SUBLANES:], b_raw[L - V7X_SUBLANES:], c_raw[L - V7X_SUBLANES:]], axis=1)

    if isinstance(g, int):
        head0 = lane0 = g * hpg
        cs2, ecs, wd, dsk = cs2_ref[...], ecs_ref[...], wd_ref[...], dsk_ref[...]
    else:
        head0, lane0 = pl.multiple_of(g * hpg, hpg), 0
        shift = (V7X_LANES - g * hpg) % V7X_LANES
        cs2, ecs, wd, dsk = (pltpu.roll(ref[...], shift, 1) for ref in (cs2_ref, ecs_ref, wd_ref, dsk_ref))
    key_t = key_ref[pl.ds(head0, hpg), :]

    xs_b = xs.astype(BF16)
    bm_b = bm.astype(BF16)
    cm_b = cm.astype(BF16)
    cb = lax.dot_general(cm_b, bm_b, (((1,), (1,)), ((), ())), preferred_element_type=F32)
    TL = V7X_LANES
    n_tiles = L // TL
    lower = (lax.broadcasted_iota(jnp.int32, (TL, TL), 0) >= lax.broadcasted_iota(jnp.int32, (TL, TL), 1))

    def head_rows(r, i):
        t0, s1 = i * TL, (i + 1) * TL
        seg = cs2[t0:s1, lane0 + r:lane0 + r + 1] - key_t[r:r + 1, :s1]
        diag = jnp.exp2(jnp.where(lower, seg[:, t0:], NEG_INF))
        lmat = diag if i == 0 else jnp.concatenate([jnp.exp2(seg[:, :t0]), diag], axis=1)
        return (cb[t0:s1, :s1] * lmat).astype(BF16)

    per_vreg = V7X_LANES // P
    lane_head = lax.broadcasted_iota(jnp.int32, (L, V7X_LANES), 1) // P
    zero = jnp.zeros((L, V7X_LANES), BF16)
    y_blocks = []
    for h0 in range(0, hpg, per_vreg):
        x_blk = xs_b[:, h0 * P:h0 * P + V7X_LANES]
        x_heads = [jnp.where(lane_head == k, x_blk, zero) for k in range(per_vreg)]
        y_rows = []
        for i in range(n_tiles):
            s1 = (i + 1) * TL
            lhs = jnp.concatenate([head_rows(h0 + k, i) for k in range(per_vreg)], axis=1)
            rhs = jnp.concatenate([xh[:s1] for xh in x_heads], axis=0)
            y_rows.append(jnp.dot(lhs, rhs, preferred_element_type=F32))
        y_blocks.append(jnp.concatenate(y_rows, axis=0))
    y = jnp.concatenate(y_blocks, axis=1)

    st = st_ref[g]
    y = y + jnp.dot(cm_b, st.astype(BF16), preferred_element_type=F32) * _expand_heads(ecs, hpg, P, lane0)
    y = y + _expand_heads(dsk, hpg, P, lane0) * xs

    xw = (xs * _expand_heads(wd, hpg, P, lane0)).astype(BF16)
    bm_t = bm.T.astype(BF16)
    st_ref[g] = (st * _expand_heads(ecs[L - 1:L, :], hpg, P, lane0)
                 + jnp.dot(bm_t, xw, preferred_element_type=F32))

    gy = y * _silu(z_ref[...])
    ms = jnp.mean(gy * gy, axis=-1, keepdims=True)
    o_ref[...] = (gy * lax.rsqrt(ms + EPS) * nw_ref[...]).astype(o_ref.dtype)


def _pad_lanes(a):
    return jnp.pad(a.reshape(1, -1), ((0, 0), (0, V7X_LANES - a.shape[0])))


def _ssd(zxbc, dt_raw, conv_w, conv_b, dt_bias, a_log, d_skip, norm_w, *, layer, batch, seq, side=()):
    T = zxbc.shape[0]
    G = SSM_GROUPS
    N = SSM_STATE
    heads = dt_bias.shape[1]
    assert heads <= V7X_LANES and dt_raw.shape[1] == V7X_LANES
    inner = heads * SSM_HEADDIM
    W = inner // G
    L = SSD_CHUNK
    nc = seq // L
    gps = SSD_GROUPS_PER_STEP
    assert G % gps == 0
    WB, NB = gps * W, gps * N
    xb, bb, cb_ = inner // WB, (2 * inner) // NB, (2 * inner + G * N) // NB
    wxb, wbb, wcb = 0, inner // NB, (inner + G * N) // NB

    dtb_p = _pad_lanes(dt_bias[layer])
    alog_p = _pad_lanes(a_log[layer])
    dsk_p = _pad_lanes(d_skip[layer])
    conv_b3 = conv_b.reshape(conv_b.shape[0], 1, -1)
    norm_w3 = norm_w.reshape(norm_w.shape[0], 1, -1)

    rows = lambda b, c, g: b * nc + c
    grid = (batch, nc, G // gps)
    s_in, s_out, s_shapes, s_args = _side_casts(side, grid)
    res = pl.pallas_call(
        _with_side_casts(functools.partial(_ssd_kernel, group_width=W), 15, 1, len(side)),
        grid=grid,
        in_specs=[
            pl.BlockSpec((L, WB), lambda b, c, g: (rows(b, c, g), g)),
            pl.BlockSpec((L, WB), lambda b, c, g: (rows(b, c, g), xb + g)),
            pl.BlockSpec((L, NB), lambda b, c, g: (rows(b, c, g), bb + g)),
            pl.BlockSpec((L, NB), lambda b, c, g: (rows(b, c, g), cb_ + g)),
            pl.BlockSpec((L, V7X_LANES), lambda b, c, g: (rows(b, c, g), 0)),
            pl.BlockSpec((None, SSM_CONV, WB), lambda b, c, g: (layer, 0, wxb + g)),
            pl.BlockSpec((None, SSM_CONV, NB), lambda b, c, g: (layer, 0, wbb + g)),
            pl.BlockSpec((None, SSM_CONV, NB), lambda b, c, g: (layer, 0, wcb + g)),
            pl.BlockSpec((None, 1, WB), lambda b, c, g: (layer, 0, wxb + g)),
            pl.BlockSpec((None, 1, NB), lambda b, c, g: (layer, 0, wbb + g)),
            pl.BlockSpec((None, 1, NB), lambda b, c, g: (layer, 0, wcb + g)),
            pl.BlockSpec((1, V7X_LANES), lambda b, c, g: (0, 0)),
            pl.BlockSpec((1, V7X_LANES), lambda b, c, g: (0, 0)),
            pl.BlockSpec((1, V7X_LANES), lambda b, c, g: (0, 0)),
            pl.BlockSpec((None, 1, WB), lambda b, c, g: (layer, 0, g)),
        ] + s_in,
        out_specs=[pl.BlockSpec((L, WB), lambda b, c, g: (rows(b, c, g), g))] + s_out,
        out_shape=[jax.ShapeDtypeStruct((T, inner), BF16)] + s_shapes,
        scratch_shapes=[
            pltpu.VMEM((G, N, W), F32),
            pltpu.VMEM((G, V7X_SUBLANES, W + 2 * N), F32),
            pltpu.VMEM((L, V7X_LANES), F32),
            pltpu.VMEM((L, V7X_LANES), F32),
            pltpu.VMEM((L, V7X_LANES), F32),
            pltpu.VMEM((V7X_LANES, L), F32),
        ],
        compiler_params=_cparams(("arbitrary", "arbitrary", "arbitrary"), 48 * 1024 * 1024),
        name="conv_ssd_scan",
    )(zxbc, zxbc, zxbc, zxbc, dt_raw, conv_w, conv_w, conv_w, conv_b3, conv_b3, conv_b3,
      dtb_p, alog_p, dsk_p, norm_w3, *s_args)
    return res[0], res[1:]


def kernel(x, ln_mix, ln_ffn, w_in_even, w_out_even, hgrn_lb, hgrn_norm, q_norm, k_norm, w_in_ssm, conv_w,
           conv_b, dt_bias, a_log, d_skip, ssm_norm, w_out_ssm, w_gate, w_up, w_down):
    batch, seq, d_model = x.shape
    depth = ln_mix.shape[0]
    T = batch * seq
    xf = x.reshape(T, d_model)

    hg_width = hgrn_lb.shape[1]
    hg_heads = hg_width // HG_DIM
    mb_width = (w_in_even.shape[2] - 4 * hg_width) // 3
    mb_heads = mb_width // MB_DIM
    ssm_heads = dt_bias.shape[1]
    ssm_main = w_in_ssm.shape[2] - ssm_heads

    def mixer_weights(layer):
        return (w_in_even, w_out_even, layer // 2) if layer % 2 == 0 else (w_in_ssm, w_out_ssm, layer // 2)

    ready = {}
    for layer in range(depth):
        w_in_l, _, idx = mixer_weights(layer)
        n_main = w_in_l.shape[2] if layer % 2 == 0 else ssm_main
        ready["in", layer] = w_in_l[idx][:, :n_main].astype(BF16)

    tm = min(MATMUL_ROW_TILE, T)
    for layer in range(depth):
        w_in_l, w_out_l, idx = mixer_weights(layer)
        ffn_side = [(w, layer) for name, w in (("gate", w_gate), ("up", w_up), ("down", w_down))
                    if (name, layer) not in ready]
        ffn_names = [name for name in ("gate", "up", "down") if (name, layer) not in ready]
        out_side = [] if ("out", layer) in ready else [(w_out_l, idx)]
        if layer % 2 == 0:
            n_in = w_in_l.shape[2]
            proj = _norm_matmul(xf, ln_mix[layer], ready["in", layer], n_cols=n_in, tm=tm,
                                tn=_col_tile(d_model, n_in))
            o_a, cast_a = _hgrn(proj, hgrn_lb, hgrn_norm[idx], batch=batch, seq=seq, n_heads=hg_heads, col0=0,
                                slot=idx, side=ffn_side)
            o_b, cast_b = _moba(proj, q_norm[idx], k_norm[idx], batch=batch, seq=seq, n_heads=mb_heads,
                                col0=4 * hg_width, side=out_side)
            mixed = [o_a, o_b]
        else:
            w_dt = jnp.pad(w_in_l[idx][:, ssm_main:], ((0, 0), (0, V7X_LANES - ssm_heads))).astype(BF16)
            zxbc, dt_raw = _norm_matmul(xf, ln_mix[layer], ready["in", layer], w_dt, n_cols=ssm_main, tm=tm,
                                        tn=_col_tile(d_model, ssm_main))
            gy, casts = _ssd(zxbc, dt_raw, conv_w, conv_b, dt_bias, a_log, d_skip, ssm_norm, layer=idx,
                             batch=batch, seq=seq, side=ffn_side + out_side)
            cast_a, cast_b = casts[:len(ffn_side)], casts[len(ffn_side):]
            mixed = [gy]
        ready.update({(name, layer): w for name, w in zip(ffn_names, cast_a)})
        if out_side:
            ready["out", layer] = cast_b[0]
        xf = _proj_residual(mixed, ready["out", layer], xf, tm=tm, tn=_col_tile(w_out_l.shape[1], d_model))

        next_side, next_names = [], []
        if layer + 1 < depth:
            _, w_out_n, idx_n = mixer_weights(layer + 1)
            next_side = [(w_out_n, idx_n), (w_down, layer + 1)]
            next_names = [("out", layer + 1), ("down", layer + 1)]
        xf, casts = _ffn(xf, ln_ffn[layer], ready["gate", layer], ready["up", layer], ready["down", layer],
                         tm=tm, tf=FFN_COL_TILE, side=next_side)
        ready.update(dict(zip(next_names, casts)))
    return xf.reshape(batch, seq, d_model)
```

```python
import functools
import math

import jax
import jax.numpy as jnp
import numpy as np
from jax import lax
from jax.experimental import pallas as pl
from jax.experimental.pallas import tpu as pltpu

F32 = jnp.float32
BF16 = jnp.bfloat16
EPS = 1e-6
NEG_INF = float("-inf")
LOG2_E = 1.4426950408889634

HG_DIM = 128
MB_DIM = 128
MB_BLOCK = 256
MB_TOPK = 3
SSM_HEADDIM = 64
SSM_STATE = 128
SSM_GROUPS = 8
SSM_CONV = 4

V7X_LANES = 128
V7X_SUBLANES = 8
V7X_VMEM_BYTES = 64 * 1024 * 1024
V7X_MXU_COLS = 256
BF16_TILE_ROWS = 16

MATMUL_ROW_TILE = 1024
WEIGHT_TILE_BYTES = 8 * 1024 * 1024
FFN_COL_TILE = 512

HGRN_CHUNK = 256
HGRN_HEADS_PER_STEP = 8
SSD_GROUPS_PER_STEP = 8
MOBA_HEADS_PER_STEP = 4
SSD_CHUNK = 256
NORM_ROWS = 128
FFN_OUT_COLS = 512


def _cparams(semantics, vmem_bytes):
    return pltpu.CompilerParams(dimension_semantics=semantics, vmem_limit_bytes=int(vmem_bytes))


def _vmem_limit(estimate_bytes):
    return min(V7X_VMEM_BYTES - 6 * 1024 * 1024, max(32 * 1024 * 1024, int(estimate_bytes * 1.25)))


def _col_tile(k_rows, n_cols):
    tiles = [t for t in range(V7X_MXU_COLS, n_cols + 1, V7X_MXU_COLS)
             if n_cols % t == 0 and k_rows * t * 2 <= WEIGHT_TILE_BYTES]
    assert tiles, (k_rows, n_cols)
    return tiles[-1]


def _side_casts(weights, grid):
    n_steps = math.prod(grid)

    def step(*ids):
        s = 0
        for k, g in zip(ids, grid):
            s = s * g + k
        return s

    in_specs, out_specs, out_shapes, args = [], [], [], []
    for arr, layer in weights:
        _, n_rows, n_cols = arr.shape
        n = max(c for c in range(1, n_steps + 1) if n_rows % c == 0 and (n_rows // c) % BF16_TILE_ROWS == 0)
        rows = n_rows // n
        in_specs.append(pl.BlockSpec(
            (None, rows, n_cols), lambda *ids, layer=layer, n=n: (layer, jnp.minimum(step(*ids), n - 1), 0)))
        out_specs.append(pl.BlockSpec((rows, n_cols), lambda *ids, n=n: (jnp.minimum(step(*ids), n - 1), 0)))
        out_shapes.append(jax.ShapeDtypeStruct((n_rows, n_cols), BF16))
        args.append(arr)
    return in_specs, out_specs, out_shapes, args


def _with_side_casts(body, n_in, n_out, n_side):
    if n_side == 0:
        return body

    def wrapped(*refs):
        side_in = refs[n_in:n_in + n_side]
        o0 = n_in + n_side
        side_out = refs[o0 + n_out:o0 + n_out + n_side]
        for src, dst in zip(side_in, side_out):
            dst[...] = src[...].astype(dst.dtype)
        body(*refs[:n_in], *refs[o0:o0 + n_out], *refs[o0 + n_out + n_side:])

    return wrapped


def _silu(x):
    h = 0.5 * x
    return h + h * jnp.tanh(h)


def _rmsnorm_rows(x_ref, gain_ref, h_ref):
    n_steps = x_ref.shape[0] // NORM_ROWS

    def body(r, carry):
        rows = pl.ds(pl.multiple_of(r * NORM_ROWS, NORM_ROWS), NORM_ROWS)
        x = x_ref[rows, :]
        ms = jnp.mean(x * x, axis=-1, keepdims=True)
        h_ref[rows, :] = (x * lax.rsqrt(ms + EPS) * gain_ref[...]).astype(h_ref.dtype)
        return carry

    lax.fori_loop(0, n_steps, body, 0)


def _norm_matmul_kernel(x_ref, gain_ref, w_ref, o_ref, h_ref):
    @pl.when(pl.program_id(1) == 0)
    def _():
        _rmsnorm_rows(x_ref, gain_ref, h_ref)

    o_ref[...] = jnp.dot(h_ref[...], w_ref[...], preferred_element_type=F32).astype(o_ref.dtype)


def _norm_matmul_extra_kernel(x_ref, gain_ref, w_ref, w2_ref, o_ref, o2_ref, h_ref, *, n_silu_tiles):
    j = pl.program_id(1)

    @pl.when(j == 0)
    def _():
        _rmsnorm_rows(x_ref, gain_ref, h_ref)
        o2_ref[...] = jnp.dot(h_ref[...], w2_ref[...], preferred_element_type=F32)

    @pl.when(j < n_silu_tiles)
    def _():
        o_ref[...] = _silu(jnp.dot(h_ref[...], w_ref[...], preferred_element_type=F32))

    @pl.when(j >= n_silu_tiles)
    def _():
        o_ref[...] = jnp.dot(h_ref[...], w_ref[...], preferred_element_type=F32)


def _norm_matmul(x, gain, w, w_extra=None, *, n_cols, tm, tn, n_silu_cols=0):
    T, D = x.shape
    assert n_silu_cols % tn == 0
    N = n_cols
    grid = (T // tm, N // tn)
    in_specs = [
        pl.BlockSpec((tm, D), lambda i, j: (i, 0)),
        pl.BlockSpec((1, D), lambda i, j: (0, 0)),
        pl.BlockSpec((D, tn), lambda i, j: (0, j)),
    ]
    out_shape = [jax.ShapeDtypeStruct((T, N), F32)]
    out_specs = [pl.BlockSpec((tm, tn), lambda i, j: (i, j))]
    args = [x, gain.reshape(1, D), w]
    est = 2 * tm * D * 4 + tm * D * 2 + 2 * D * tn * 2 + 3 * tm * tn * 4
    if w_extra is None:
        body = _norm_matmul_kernel
    else:
        body = functools.partial(_norm_matmul_extra_kernel, n_silu_tiles=n_silu_cols // tn)
        n2 = w_extra.shape[1]
        in_specs.append(pl.BlockSpec((D, n2), lambda i, j: (0, 0)))
        out_shape.append(jax.ShapeDtypeStruct((T, n2), F32))
        out_specs.append(pl.BlockSpec((tm, n2), lambda i, j: (i, 0)))
        args.append(w_extra)
        est += 2 * D * n2 * 2 + 2 * tm * n2 * 4
    res = pl.pallas_call(
        body,
        grid=grid,
        in_specs=in_specs,
        out_specs=out_specs,
        out_shape=out_shape,
        scratch_shapes=[pltpu.VMEM((tm, D), BF16)],
        compiler_params=_cparams(("parallel", "arbitrary"), _vmem_limit(est)),
        name="norm_in_proj",
    )(*args)
    return res[0] if w_extra is None else res


def _proj_residual_kernel(*refs, n_lhs):
    lhs_refs = refs[:n_lhs]
    w_ref, x_ref, o_ref = refs[n_lhs:]
    acc = x_ref[...]
    k0 = 0
    for a_ref in lhs_refs:
        k1 = k0 + a_ref.shape[1]
        acc = acc + jnp.dot(a_ref[...], w_ref[k0:k1, :], preferred_element_type=F32)
        k0 = k1
    o_ref[...] = acc


def _proj_residual(lhs_list, w, x, *, tm, tn):
    T, N = x.shape
    K = w.shape[0]
    grid = (T // tm, N // tn)
    in_specs = [pl.BlockSpec((tm, a.shape[1]), lambda i, j: (i, 0)) for a in lhs_list]
    in_specs += [
        pl.BlockSpec((K, tn), lambda i, j: (0, j)),
        pl.BlockSpec((tm, tn), lambda i, j: (i, j)),
    ]
    est = 2 * tm * K * 2 + 2 * K * tn * 2 + 5 * tm * tn * 4
    return pl.pallas_call(
        functools.partial(_proj_residual_kernel, n_lhs=len(lhs_list)),
        grid=grid,
        in_specs=in_specs,
        out_specs=pl.BlockSpec((tm, tn), lambda i, j: (i, j)),
        out_shape=jax.ShapeDtypeStruct((T, N), F32),
        compiler_params=_cparams(("parallel", "arbitrary"), _vmem_limit(est)),
        name="out_proj_residual",
    )(*lhs_list, w, x)


def _ffn_kernel(x_ref, gain_ref, wg_ref, wu_ref, wd_ref, o_ref, h_ref):
    f = pl.program_id(1)

    @pl.when(f == 0)
    def _():
        _rmsnorm_rows(x_ref, gain_ref, h_ref)

    h = h_ref[...]
    g = jnp.dot(h, wg_ref[...], preferred_element_type=F32)
    u = jnp.dot(h, wu_ref[...], preferred_element_type=F32)
    a = (_silu(g) * u).astype(BF16)
    n_out = o_ref.shape[1]

    def down(c0):
        cols = slice(c0, min(c0 + FFN_OUT_COLS, n_out))
        return cols, jnp.dot(a, wd_ref[:, cols], preferred_element_type=F32)

    @pl.when(f == 0)
    def _():
        for c0 in range(0, n_out, FFN_OUT_COLS):
            cols, d = down(c0)
            o_ref[:, cols] = x_ref[:, cols] + d

    @pl.when(f != 0)
    def _():
        for c0 in range(0, n_out, FFN_OUT_COLS):
            cols, d = down(c0)
            o_ref[:, cols] += d


def _ffn(x, gain, wg, wu, wd, *, tm, tf, side=()):
    T, D = x.shape
    F = wg.shape[1]
    grid = (T // tm, F // tf)
    s_in, s_out, s_shapes, s_args = _side_casts(side, grid)
    est = 4 * tm * D * 4 + tm * D * 2 + 3 * 2 * D * tf * 2 + 3 * tm * tf * 4 + tm * FFN_OUT_COLS * 4
    res = pl.pallas_call(
        _with_side_casts(_ffn_kernel, 5, 1, len(side)),
        grid=grid,
        in_specs=[
            pl.BlockSpec((tm, D), lambda i, f: (i, 0)),
            pl.BlockSpec((1, D), lambda i, f: (0, 0)),
            pl.BlockSpec((D, tf), lambda i, f: (0, f)),
            pl.BlockSpec((D, tf), lambda i, f: (0, f)),
            pl.BlockSpec((tf, D), lambda i, f: (f, 0)),
        ] + s_in,
        out_specs=[pl.BlockSpec((tm, D), lambda i, f: (i, 0))] + s_out,
        out_shape=[jax.ShapeDtypeStruct((T, D), F32)] + s_shapes,
        scratch_shapes=[pltpu.VMEM((tm, D), BF16)],
        compiler_params=_cparams(("arbitrary", "arbitrary"), _vmem_limit(est)),
        name="swiglu_ffn",
    )(x, gain.reshape(1, D), wg, wu, wd, *s_args)
    return res[0], res[1:]


def _cumsum_rows(a):
    R, D = a.shape
    S8 = V7X_SUBLANES
    a3 = a.reshape(R // S8, S8, D)
    sub = lax.broadcasted_iota(jnp.int32, a3.shape, 1)
    shift = 1
    while shift < S8:
        a3 = a3 + jnp.where(sub >= shift, pltpu.roll(a3, shift, 1), 0.0)
        shift *= 2
    groups = []
    carry = None
    for g in range(R // S8):
        grp = a3[g]
        if carry is not None:
            grp = grp + carry
        groups.append(grp)
        carry = grp[S8 - 1:S8]
    return jnp.concatenate(groups, axis=0)


def _hgrn_level_table(n):
    t = np.arange(n)[:, None]
    s = np.arange(n)[None, :]
    lv = np.floor(np.log2(np.maximum(t ^ s, 1))).astype(np.int32)
    return jnp.asarray(np.where(t > s, lv, np.where(t == s, -1, -2)).astype(np.int32))


def _hgrn_kernel(q_ref, f_ref, i_ref, g_ref, lb_ref, nw_ref, lv_ref, o_ref, st_ref, *, slot):
    @pl.when(pl.program_id(2) == 0)
    def _():
        st_ref[...] = jnp.zeros_like(st_ref)

    for hh in range(q_ref.shape[1] // HG_DIM):
        cols = slice(hh * HG_DIM, (hh + 1) * HG_DIM)
        _hgrn_head(q_ref, f_ref, i_ref, g_ref, lb_ref, nw_ref, lv_ref, o_ref, st_ref.at[hh], cols, slot)


def _hgrn_head(q_ref, f_ref, i_ref, g_ref, lb_ref, nw_ref, lv_ref, o_ref, st_ref, cols, slot):
    C, D = q_ref.shape[0], HG_DIM
    H = lv_ref.shape[0]
    S8 = V7X_SUBLANES

    lbl = lb_ref[:, cols]
    ex = jnp.exp(lbl - jnp.max(lbl, axis=0, keepdims=True))
    lb = jnp.sum(ex[: slot + 1], axis=0, keepdims=True) / jnp.sum(ex, axis=0, keepdims=True)

    th = jnp.tanh(0.5 * f_ref[:, cols])
    q = _silu(q_ref[:, cols])
    log_f = jnp.log(lb + (1.0 - lb) * (0.5 + 0.5 * th))
    k = (1.0 - lb) * (0.5 - 0.5 * th)
    v = i_ref[:, cols]
    v_b = v.astype(BF16)

    b = _cumsum_rows(log_f) * LOG2_E
    row = lax.broadcasted_iota(jnp.int32, (C, D), 0)
    b8 = b.reshape(C // S8, S8, D)

    def bcast_row(r):
        return jnp.broadcast_to(b8[:, r:r + 1, :], b8.shape).reshape(C, D)

    lv = lv_ref[...]
    diag = jnp.sum(q * k, axis=1, keepdims=True)
    tiles = [jnp.where(lv == -1, diag[j * H:(j + 1) * H], 0.0) for j in range(C // H)]
    cross = None

    n_levels = int(math.log2(C))
    for lvl in range(n_levels):
        half = 1 << lvl
        blk = 2 * half
        is_q = (row & half) != 0
        if blk == 2:
            beta = jnp.where(is_q, pltpu.roll(b, 1, 0), b)
        elif blk < S8:
            beta = bcast_row(half - 1)
            for r0 in range(blk, S8, blk):
                beta = jnp.where((row & (S8 - 1)) >= r0, bcast_row(r0 + half - 1), beta)
        else:
            mid = b.reshape(C // blk, blk, D)[:, half - 1:half, :]
            beta = jnp.broadcast_to(mid, (C // blk, blk, D)).reshape(C, D)
        x = (jnp.where(is_q, q, k) * jnp.exp2(-jnp.abs(b - beta))).astype(BF16)
        if blk <= H:
            for j in range(C // H):
                xj = x[j * H:(j + 1) * H]
                p = lax.dot_general(xj, xj, (((1,), (1,)), ((), ())), preferred_element_type=F32)
                tiles[j] = jnp.where(lv == lvl, p, tiles[j])
        else:
            cross = lax.dot_general(x[H:], x[:H], (((1,), (1,)), ((), ())), preferred_element_type=F32)

    st = st_ref[...]
    o_top = jnp.dot(tiles[0].astype(BF16), v_b[:H], preferred_element_type=F32)
    o_bot = jnp.dot(jnp.concatenate([cross, tiles[1]], axis=1).astype(BF16), v_b, preferred_element_type=F32)
    qe = (q * jnp.exp2(b)).astype(BF16)
    o = jnp.concatenate([o_top, o_bot], axis=0) + lax.dot_general(
        qe, st.astype(BF16), (((1,), (1,)), ((), ())), preferred_element_type=F32)

    b_last = b[C - 1:C, :]
    k_dec = (k * jnp.exp2(b_last - b)).astype(BF16)
    v_t = v.T.astype(BF16)
    st_ref[...] = st * jnp.exp2(b_last) + jnp.dot(v_t, k_dec, preferred_element_type=F32)

    ms = jnp.mean(o * o, axis=-1, keepdims=True)
    o = o * lax.rsqrt(ms + EPS) * nw_ref[...] * _silu(g_ref[:, cols])
    o_ref[:, cols] = o.astype(o_ref.dtype)


def _hgrn(proj, lb_table, norm_w, *, batch, seq, n_heads, col0, slot, side=()):
    T = proj.shape[0]
    C = HGRN_CHUNK
    H = C // 2
    nc = seq // C
    hps = HGRN_HEADS_PER_STEP
    wb = hps * HG_DIM
    assert n_heads % hps == 0 and col0 % wb == 0 and (n_heads * HG_DIM) % wb == 0
    n_slots = lb_table.shape[0]

    def sec(s):
        cb = (col0 + s * n_heads * HG_DIM) // wb
        return pl.BlockSpec((C, wb), lambda b, h, c: (b * nc + c, cb + h))

    grid = (batch, n_heads // hps, nc)
    s_in, s_out, s_shapes, s_args = _side_casts(side, grid)
    res = pl.pallas_call(
        _with_side_casts(functools.partial(_hgrn_kernel, slot=slot), 7, 1, len(side)),
        grid=grid,
        in_specs=[
            sec(0), sec(1), sec(2), sec(3),
            pl.BlockSpec((n_slots, wb), lambda b, h, c: (0, h)),
            pl.BlockSpec((1, HG_DIM), lambda b, h, c: (0, 0)),
            pl.BlockSpec((H, H), lambda b, h, c: (0, 0)),
        ] + s_in,
        out_specs=[pl.BlockSpec((C, wb), lambda b, h, c: (b * nc + c, h))] + s_out,
        out_shape=[jax.ShapeDtypeStruct((T, n_heads * HG_DIM), BF16)] + s_shapes,
        scratch_shapes=[pltpu.VMEM((hps, HG_DIM, HG_DIM), F32)],
        compiler_params=_cparams(("arbitrary", "arbitrary", "arbitrary"), 40 * 1024 * 1024),
        name="hgrn2_scan",
    )(proj, proj, proj, proj, lb_table, norm_w.reshape(1, HG_DIM), _hgrn_level_table(H), *s_args)
    return res[0], res[1:]


def _moba_kernel(q_ref, k_ref, v_ref, qw_ref, kw_ref, o_ref):
    for hh in range(q_ref.shape[1] // MB_DIM):
        cols = slice(hh * MB_DIM, (hh + 1) * MB_DIM)
        _moba_head(q_ref.at[:, cols], k_ref.at[:, cols], v_ref.at[:, cols], qw_ref, kw_ref, o_ref.at[:, cols])


def _moba_head(q_ref, k_ref, v_ref, qw_ref, kw_ref, o_ref):
    S, D = q_ref.shape
    BLK = MB_BLOCK
    nb = S // BLK
    scale = D ** -0.5

    q = q_ref[...]
    k = k_ref[...]
    qn = q * lax.rsqrt(jnp.mean(q * q, axis=-1, keepdims=True) + EPS) * qw_ref[...]
    kn = k * lax.rsqrt(jnp.mean(k * k, axis=-1, keepdims=True) + EPS) * kw_ref[...]
    k_mean = jnp.mean(kn.reshape(nb, BLK, D), axis=1)
    qn_t = qn.T
    gate_t = jnp.dot(k_mean, qn_t, precision=lax.Precision.HIGHEST,
                     preferred_element_type=F32)
    qs_t = (qn_t * (scale * LOG2_E)).astype(BF16)
    kn_b = kn.astype(BF16)
    v_t = v_ref[...].T.astype(BF16)

    kk = lax.broadcasted_iota(jnp.int32, (BLK, BLK), 0)
    qq = lax.broadcasted_iota(jnp.int32, (BLK, BLK), 1)
    causal = kk <= qq

    for j in range(nb):
        cols = slice(j * BLK, (j + 1) * BLK)
        q_j = qs_t[:, cols]
        n_keep = min(MB_TOPK, j)
        selected = None
        if j > n_keep:
            g = [gate_t[m:m + 1, cols] for m in range(j)]
            selected = []
            for n in range(j):
                beaten = jnp.zeros((1, BLK), jnp.int32)
                for m in range(j):
                    if m < n:
                        beaten = beaten + (g[m] >= g[n]).astype(jnp.int32)
                    elif m > n:
                        beaten = beaten + (g[m] > g[n]).astype(jnp.int32)
                selected.append(beaten < n_keep)
        scores = []
        for n in range(j + 1):
            s = jnp.dot(kn_b[n * BLK:(n + 1) * BLK, :], q_j, preferred_element_type=F32)
            if n == j:
                s = jnp.where(causal, s, NEG_INF)
            elif selected is not None:
                s = jnp.where(selected[n], s, NEG_INF)
            scores.append(s)
        m_run = jnp.max(scores[0], axis=0, keepdims=True)
        for s in scores[1:]:
            m_run = jnp.maximum(m_run, jnp.max(s, axis=0, keepdims=True))
        denom = jnp.zeros((1, BLK), F32)
        acc = jnp.zeros((D, BLK), F32)
        for n, s in enumerate(scores):
            p = jnp.exp2(s - m_run)
            denom = denom + jnp.sum(p, axis=0, keepdims=True)
            acc = acc + jnp.dot(v_t[:, n * BLK:(n + 1) * BLK], p.astype(BF16), preferred_element_type=F32)
        o_ref[j * BLK:(j + 1) * BLK, :] = (acc / denom).T.astype(o_ref.dtype)


def _moba(proj, q_norm_w, k_norm_w, *, batch, seq, n_heads, col0, side=()):
    T = proj.shape[0]
    hps = MOBA_HEADS_PER_STEP
    wb = hps * MB_DIM
    assert n_heads % hps == 0 and col0 % wb == 0 and (n_heads * MB_DIM) % wb == 0

    def sec(s):
        cb = (col0 + s * n_heads * MB_DIM) // wb
        return pl.BlockSpec((seq, wb), lambda b, h: (b, cb + h))

    grid = (batch, n_heads // hps)
    s_in, s_out, s_shapes, s_args = _side_casts(side, grid)
    res = pl.pallas_call(
        _with_side_casts(_moba_kernel, 5, 1, len(side)),
        grid=grid,
        in_specs=[
            sec(0), sec(1), sec(2),
            pl.BlockSpec((1, MB_DIM), lambda b, h: (0, 0)),
            pl.BlockSpec((1, MB_DIM), lambda b, h: (0, 0)),
        ] + s_in,
        out_specs=[pl.BlockSpec((seq, wb), lambda b, h: (b, h))] + s_out,
        out_shape=[jax.ShapeDtypeStruct((T, n_heads * MB_DIM), BF16)] + s_shapes,
        compiler_params=_cparams(("arbitrary", "arbitrary"), 48 * 1024 * 1024),
        name="moba_attention",
    )(proj, proj, proj, q_norm_w.reshape(1, MB_DIM), k_norm_w.reshape(1, MB_DIM), *s_args)
    return res[0], res[1:]


def _causal_conv_silu(raw, tail, w_ref, bias_ref):
    K = SSM_CONV
    acc = raw * w_ref[K - 1:K, :] + bias_ref[...]
    row8 = lax.broadcasted_iota(jnp.int32, tail.shape, 0)
    for d in range(1, K):
        sh = pltpu.roll(raw, d, 0)
        head = jnp.where(row8 < d, pltpu.roll(tail, d, 0), sh[:V7X_SUBLANES])
        sh = jnp.concatenate([head, sh[V7X_SUBLANES:]], axis=0)
        acc = acc + sh * w_ref[K - 1 - d:K - d, :]
    return _silu(acc)


def _expand_heads(cols, n_heads, width, lane0=0):
    rows = cols.shape[0]
    per_vreg = V7X_LANES // width
    lane = lax.broadcasted_iota(jnp.int32, (rows, V7X_LANES), 1)
    blocks = []
    for h0 in range(lane0, lane0 + n_heads, per_vreg):
        blk = jnp.broadcast_to(cols[:, h0:h0 + 1], (rows, V7X_LANES))
        for k in range(1, per_vreg):
            nxt = jnp.broadcast_to(cols[:, h0 + k:h0 + k + 1], (rows, V7X_LANES))
            blk = jnp.where(lane >= k * width, nxt, blk)
        blocks.append(blk)
    return jnp.concatenate(blocks, axis=1)


def _ssd_kernel(z_ref, x_ref, bm_ref, cm_ref, dt_ref, cwx_ref, cwb_ref, cwc_ref, cbx_ref, cbb_ref, cbc_ref,
                dtb_ref, alog_ref, dsk_ref, nw_ref, o_ref, st_ref, tail_ref, cs2_ref, ecs_ref, wd_ref, key_ref,
                *, group_width):
    L = x_ref.shape[0]
    W, N = group_width, SSM_STATE
    gps = x_ref.shape[1] // W
    all_groups = gps == st_ref.shape[0]

    @pl.when(pl.program_id(2) == 0)
    def _():
        pre = dt_ref[...] + dtb_ref[...]
        dt_all = jnp.maximum(pre, 0.0) + jnp.log1p(jnp.exp(-jnp.abs(pre)))
        a_all = dt_all * (-jnp.exp(alog_ref[...]))
        cs_all = _cumsum_rows(a_all)
        cs2_all = cs_all * LOG2_E
        cs2_ref[...] = cs2_all
        ecs_ref[...] = jnp.exp(cs_all)
        wd_ref[...] = jnp.exp(cs_all[L - 1:L, :] - cs_all) * dt_all
        key_ref[...] = (cs2_all - jnp.log2(dt_all)).T

    for gg in range(gps):
        cw = slice(gg * W, (gg + 1) * W)
        cn = slice(gg * N, (gg + 1) * N)
        _ssd_group(z_ref.at[:, cw], x_ref.at[:, cw], bm_ref.at[:, cn], cm_ref.at[:, cn],
                   cwx_ref.at[:, cw], cwb_ref.at[:, cn], cwc_ref.at[:, cn],
                   cbx_ref.at[:, cw], cbb_ref.at[:, cn], cbc_ref.at[:, cn],
                   dsk_ref, nw_ref.at[:, cw], o_ref.at[:, cw], st_ref, tail_ref, cs2_ref, ecs_ref, wd_ref, key_ref,
                   gg if all_groups else pl.program_id(2) * gps + gg)


def _ssd_group(z_ref, x_ref, bm_ref, cm_ref, cwx_ref, cwb_ref, cwc_ref, cbx_ref, cbb_ref, cbc_ref,
               dsk_ref, nw_ref, o_ref, st_ref, tail_ref, cs2_ref, ecs_ref, wd_ref, key_ref, g):
    L, W = x_ref.shape
    P = SSM_HEADDIM
    hpg = W // P
    N = bm_ref.shape[1]

    @pl.when(pl.program_id(1) == 0)
    def _():
        st_ref[g] = jnp.zeros(st_ref.shape[1:], F32)
        tail_ref[g] = jnp.zeros(tail_ref.shape[1:], F32)

    x_raw = x_ref[...]
    b_raw = bm_ref[...]
    c_raw = cm_ref[...]
    tail = tail_ref[g]
    xs = _causal_conv_silu(x_raw, tail[:, :W], cwx_ref, cbx_ref)
    bm = _causal_conv_silu(b_raw, tail[:, W:W + N], cwb_ref, cbb_ref)
    cm = _causal_conv_silu(c_raw, tail[:, W + N:], cwc_ref, cbc_ref)
    tail_ref[g] = jnp.concatenate(
        [x_raw[L - V7X_SUBLANES:], b_raw[L - V7X_SUBLANES:], c_raw[L - V7X_SUBLANES:]], axis=1)

    if isinstance(g, int):
        head0 = lane0 = g * hpg
        cs2, ecs, wd, dsk = cs2_ref[...], ecs_ref[...], wd_ref[...], dsk_ref[...]
    else:
        head0, lane0 = pl.multiple_of(g * hpg, hpg), 0
        shift = (V7X_LANES - g * hpg) % V7X_LANES
        cs2, ecs, wd, dsk = (pltpu.roll(ref[...], shift, 1) for ref in (cs2_ref, ecs_ref, wd_ref, dsk_ref))
    key_t = key_ref[pl.ds(head0, hpg), :]

    xs_b = xs.astype(BF16)
    bm_b = bm.astype(BF16)
    cm_b = cm.astype(BF16)
    cb = lax.dot_general(cm_b, bm_b, (((1,), (1,)), ((), ())), preferred_element_type=F32)
    TL = V7X_LANES
    n_tiles = L // TL
    lower = (lax.broadcasted_iota(jnp.int32, (TL, TL), 0) >= lax.broadcasted_iota(jnp.int32, (TL, TL), 1))

    def head_rows(r, i):
        t0, s1 = i * TL, (i + 1) * TL
        seg = cs2[t0:s1, lane0 + r:lane0 + r + 1] - key_t[r:r + 1, :s1]
        diag = jnp.exp2(jnp.where(lower, seg[:, t0:], NEG_INF))
        lmat = diag if i == 0 else jnp.concatenate([jnp.exp2(seg[:, :t0]), diag], axis=1)
        return (cb[t0:s1, :s1] * lmat).astype(BF16)

    per_vreg = V7X_LANES // P
    lane_head = lax.broadcasted_iota(jnp.int32, (L, V7X_LANES), 1) // P
    zero = jnp.zeros((L, V7X_LANES), BF16)
    y_blocks = []
    for h0 in range(0, hpg, per_vreg):
        x_blk = xs_b[:, h0 * P:h0 * P + V7X_LANES]
        x_heads = [jnp.where(lane_head == k, x_blk, zero) for k in range(per_vreg)]
        y_rows = []
        for i in range(n_tiles):
            s1 = (i + 1) * TL
            lhs = jnp.concatenate([head_rows(h0 + k, i) for k in range(per_vreg)], axis=1)
            rhs = jnp.concatenate([xh[:s1] for xh in x_heads], axis=0)
            y_rows.append(jnp.dot(lhs, rhs, preferred_element_type=F32))
        y_blocks.append(jnp.concatenate(y_rows, axis=0))
    y = jnp.concatenate(y_blocks, axis=1)

    st = st_ref[g]
    y = y + jnp.dot(cm_b, st.astype(BF16), preferred_element_type=F32) * _expand_heads(ecs, hpg, P, lane0)
    y = y + _expand_heads(dsk, hpg, P, lane0) * xs

    xw = (xs * _expand_heads(wd, hpg, P, lane0)).astype(BF16)
    bm_t = bm.T.astype(BF16)
    st_ref[g] = (st * _expand_heads(ecs[L - 1:L, :], hpg, P, lane0)
                 + jnp.dot(bm_t, xw, preferred_element_type=F32))

    gy = y * z_ref[...]
    ms = jnp.mean(gy * gy, axis=-1, keepdims=True)
    o_ref[...] = (gy * lax.rsqrt(ms + EPS) * nw_ref[...]).astype(o_ref.dtype)


def _pad_lanes(a):
    return jnp.pad(a.reshape(1, -1), ((0, 0), (0, V7X_LANES - a.shape[0])))


def _ssd(zxbc, dt_raw, conv_w, conv_b, dt_bias, a_log, d_skip, norm_w, *, layer, batch, seq, side=()):
    T = zxbc.shape[0]
    G = SSM_GROUPS
    N = SSM_STATE
    heads = dt_bias.shape[1]
    assert heads <= V7X_LANES and dt_raw.shape[1] == V7X_LANES
    inner = heads * SSM_HEADDIM
    W = inner // G
    L = SSD_CHUNK
    nc = seq // L
    gps = SSD_GROUPS_PER_STEP
    assert G % gps == 0
    WB, NB = gps * W, gps * N
    xb, bb, cb_ = inner // WB, (2 * inner) // NB, (2 * inner + G * N) // NB
    wxb, wbb, wcb = 0, inner // NB, (inner + G * N) // NB

    dtb_p = _pad_lanes(dt_bias[layer])
    alog_p = _pad_lanes(a_log[layer])
    dsk_p = _pad_lanes(d_skip[layer])
    conv_b3 = conv_b.reshape(conv_b.shape[0], 1, -1)
    norm_w3 = norm_w.reshape(norm_w.shape[0], 1, -1)

    rows = lambda b, c, g: b * nc + c
    grid = (batch, nc, G // gps)
    s_in, s_out, s_shapes, s_args = _side_casts(side, grid)
    res = pl.pallas_call(
        _with_side_casts(functools.partial(_ssd_kernel, group_width=W), 15, 1, len(side)),
        grid=grid,
        in_specs=[
            pl.BlockSpec((L, WB), lambda b, c, g: (rows(b, c, g), g)),
            pl.BlockSpec((L, WB), lambda b, c, g: (rows(b, c, g), xb + g)),
            pl.BlockSpec((L, NB), lambda b, c, g: (rows(b, c, g), bb + g)),
            pl.BlockSpec((L, NB), lambda b, c, g: (rows(b, c, g), cb_ + g)),
            pl.BlockSpec((L, V7X_LANES), lambda b, c, g: (rows(b, c, g), 0)),
            pl.BlockSpec((None, SSM_CONV, WB), lambda b, c, g: (layer, 0, wxb + g)),
            pl.BlockSpec((None, SSM_CONV, NB), lambda b, c, g: (layer, 0, wbb + g)),
            pl.BlockSpec((None, SSM_CONV, NB), lambda b, c, g: (layer, 0, wcb + g)),
            pl.BlockSpec((None, 1, WB), lambda b, c, g: (layer, 0, wxb + g)),
            pl.BlockSpec((None, 1, NB), lambda b, c, g: (layer, 0, wbb + g)),
            pl.BlockSpec((None, 1, NB), lambda b, c, g: (layer, 0, wcb + g)),
            pl.BlockSpec((1, V7X_LANES), lambda b, c, g: (0, 0)),
            pl.BlockSpec((1, V7X_LANES), lambda b, c, g: (0, 0)),
            pl.BlockSpec((1, V7X_LANES), lambda b, c, g: (0, 0)),
            pl.BlockSpec((None, 1, WB), lambda b, c, g: (layer, 0, g)),
        ] + s_in,
        out_specs=[pl.BlockSpec((L, WB), lambda b, c, g: (rows(b, c, g), g))] + s_out,
        out_shape=[jax.ShapeDtypeStruct((T, inner), BF16)] + s_shapes,
        scratch_shapes=[
            pltpu.VMEM((G, N, W), F32),
            pltpu.VMEM((G, V7X_SUBLANES, W + 2 * N), F32),
            pltpu.VMEM((L, V7X_LANES), F32),
            pltpu.VMEM((L, V7X_LANES), F32),
            pltpu.VMEM((L, V7X_LANES), F32),
            pltpu.VMEM((V7X_LANES, L), F32),
        ],
        compiler_params=_cparams(("arbitrary", "arbitrary", "arbitrary"), 48 * 1024 * 1024),
        name="conv_ssd_scan",
    )(zxbc, zxbc, zxbc, zxbc, dt_raw, conv_w, conv_w, conv_w, conv_b3, conv_b3, conv_b3,
      dtb_p, alog_p, dsk_p, norm_w3, *s_args)
    return res[0], res[1:]


def kernel(x, ln_mix, ln_ffn, w_in_even, w_out_even, hgrn_lb, hgrn_norm, q_norm, k_norm, w_in_ssm, conv_w,
           conv_b, dt_bias, a_log, d_skip, ssm_norm, w_out_ssm, w_gate, w_up, w_down):
    batch, seq, d_model = x.shape
    depth = ln_mix.shape[0]
    T = batch * seq
    xf = x.reshape(T, d_model)

    hg_width = hgrn_lb.shape[1]
    hg_heads = hg_width // HG_DIM
    mb_width = (w_in_even.shape[2] - 4 * hg_width) // 3
    mb_heads = mb_width // MB_DIM
    ssm_heads = dt_bias.shape[1]
    ssm_main = w_in_ssm.shape[2] - ssm_heads

    def mixer_weights(layer):
        return (w_in_even, w_out_even, layer // 2) if layer % 2 == 0 else (w_in_ssm, w_out_ssm, layer // 2)

    ready = {}
    for layer in range(depth):
        w_in_l, _, idx = mixer_weights(layer)
        ready["in", layer] = w_in_l[idx].astype(BF16)

    tm = min(MATMUL_ROW_TILE, T)
    for layer in range(depth):
        w_in_l, w_out_l, idx = mixer_weights(layer)
        ffn_side = [(w, layer) for name, w in (("gate", w_gate), ("up", w_up), ("down", w_down))
                    if (name, layer) not in ready]
        ffn_names = [name for name in ("gate", "up", "down") if (name, layer) not in ready]
        out_side = [] if ("out", layer) in ready else [(w_out_l, idx)]
        if layer % 2 == 0:
            n_in = w_in_l.shape[2]
            proj = _norm_matmul(xf, ln_mix[layer], ready["in", layer], n_cols=n_in, tm=tm,
                                tn=_col_tile(d_model, n_in))
            o_a, cast_a = _hgrn(proj, hgrn_lb, hgrn_norm[idx], batch=batch, seq=seq, n_heads=hg_heads, col0=0,
                                slot=idx, side=ffn_side)
            o_b, cast_b = _moba(proj, q_norm[idx], k_norm[idx], batch=batch, seq=seq, n_heads=mb_heads,
                                col0=4 * hg_width, side=out_side)
            mixed = [o_a, o_b]
        else:
            w_dt = jnp.pad(w_in_l[idx][:, ssm_main:], ((0, 0), (0, V7X_LANES - ssm_heads))).astype(BF16)
            zxbc, dt_raw = _norm_matmul(xf, ln_mix[layer], ready["in", layer], w_dt, n_cols=ssm_main, tm=tm,
                                        tn=_col_tile(d_model, ssm_main), n_silu_cols=ssm_heads * SSM_HEADDIM)
            gy, casts = _ssd(zxbc, dt_raw, conv_w, conv_b, dt_bias, a_log, d_skip, ssm_norm, layer=idx,
                             batch=batch, seq=seq, side=ffn_side + out_side)
            cast_a, cast_b = casts[:len(ffn_side)], casts[len(ffn_side):]
            mixed = [gy]
        ready.update({(name, layer): w for name, w in zip(ffn_names, cast_a)})
        if out_side:
            ready["out", layer] = cast_b[0]
        xf = _proj_residual(mixed, ready["out", layer], xf, tm=tm, tn=_col_tile(w_out_l.shape[1], d_model))

        next_side, next_names = [], []
        if layer + 1 < depth:
            _, w_out_n, idx_n = mixer_weights(layer + 1)
            next_side = [(w_out_n, idx_n), (w_down, layer + 1)]
            next_names = [("out", layer + 1), ("down", layer + 1)]
        xf, casts = _ffn(xf, ln_ffn[layer], ready["gate", layer], ready["up", layer], ready["down", layer],
                         tm=tm, tf=FFN_COL_TILE, side=next_side)
        ready.update(dict(zip(next_names, casts)))
    return xf.reshape(batch, seq, d_model)
```

```python
import functools
import math

import jax
import jax.numpy as jnp
import numpy as np
from jax import lax
from jax.experimental import pallas as pl
from jax.experimental.pallas import tpu as pltpu

F32 = jnp.float32
BF16 = jnp.bfloat16
EPS = 1e-6
NEG_INF = float("-inf")
LOG2_E = 1.4426950408889634

HG_DIM = 128
MB_DIM = 128
MB_BLOCK = 256
MB_TOPK = 3
SSM_HEADDIM = 64
SSM_STATE = 128
SSM_GROUPS = 8
SSM_CONV = 4

V7X_LANES = 128
V7X_SUBLANES = 8
V7X_VMEM_BYTES = 64 * 1024 * 1024
V7X_DEFAULT_SCOPED_VMEM = 32 * 1024 * 1024
V7X_MXU_COLS = 256
BF16_TILE_ROWS = 16

VMEM_RESERVE_BYTES = 6 * 1024 * 1024
MIXER_VMEM_BYTES = 48 * 1024 * 1024

MATMUL_ROW_TILE = 1024
WEIGHT_TILE_BYTES = 8 * 1024 * 1024
FFN_COL_TILE = 512

HGRN_CHUNK = 256
HGRN_HEADS_PER_STEP = 8
SSD_GROUPS_PER_STEP = 8
MOBA_HEADS_PER_STEP = 4
SSD_CHUNK = 256
CONV_ROW_TILE = 64
NORM_ROWS = 256
FFN_OUT_COLS = 512


def _cparams(semantics, vmem_bytes):
    return pltpu.CompilerParams(dimension_semantics=semantics, vmem_limit_bytes=int(vmem_bytes))


def _vmem_limit(estimate_bytes):
    return min(V7X_VMEM_BYTES - VMEM_RESERVE_BYTES, max(V7X_DEFAULT_SCOPED_VMEM, int(estimate_bytes * 1.25)))


def _col_tile(k_rows, n_cols):
    tiles = [t for t in range(V7X_MXU_COLS, n_cols + 1, V7X_MXU_COLS)
             if n_cols % t == 0 and k_rows * t * 2 <= WEIGHT_TILE_BYTES]
    assert tiles, (k_rows, n_cols)
    return tiles[-1]


def _side_casts(weights, grid):
    n_steps = math.prod(grid)

    def step(*ids):
        s = 0
        for k, g in zip(ids, grid):
            s = s * g + k
        return s

    in_specs, out_specs, out_shapes, args = [], [], [], []
    for arr, layer in weights:
        _, n_rows, n_cols = arr.shape
        n = max(c for c in range(1, n_steps + 1) if n_rows % c == 0 and (n_rows // c) % BF16_TILE_ROWS == 0)
        rows = n_rows // n
        in_specs.append(pl.BlockSpec(
            (None, rows, n_cols), lambda *ids, layer=layer, n=n: (layer, jnp.minimum(step(*ids), n - 1), 0)))
        out_specs.append(pl.BlockSpec((rows, n_cols), lambda *ids, n=n: (jnp.minimum(step(*ids), n - 1), 0)))
        out_shapes.append(jax.ShapeDtypeStruct((n_rows, n_cols), BF16))
        args.append(arr)
    return in_specs, out_specs, out_shapes, args


def _with_side_casts(body, n_in, n_out, n_side):
    if n_side == 0:
        return body

    def wrapped(*refs):
        side_in = refs[n_in:n_in + n_side]
        o0 = n_in + n_side
        side_out = refs[o0 + n_out:o0 + n_out + n_side]
        for src, dst in zip(side_in, side_out):
            dst[...] = src[...].astype(dst.dtype)
        body(*refs[:n_in], *refs[o0:o0 + n_out], *refs[o0 + n_out + n_side:])

    return wrapped


def _silu(x):
    h = 0.5 * x
    return h + h * jnp.tanh(h)


def _rmsnorm_rows(x_ref, gain_ref, h_ref):
    n_steps = x_ref.shape[0] // NORM_ROWS

    def body(r, carry):
        rows = pl.ds(pl.multiple_of(r * NORM_ROWS, NORM_ROWS), NORM_ROWS)
        x = x_ref[rows, :]
        ms = jnp.mean(x * x, axis=-1, keepdims=True)
        h_ref[rows, :] = (x * lax.rsqrt(ms + EPS) * gain_ref[...]).astype(h_ref.dtype)
        return carry

    lax.fori_loop(0, n_steps, body, 0)


def _norm_matmul_kernel(x_ref, gain_ref, w_ref, o_ref, h_ref):
    @pl.when(pl.program_id(1) == 0)
    def _():
        _rmsnorm_rows(x_ref, gain_ref, h_ref)

    o_ref[...] = jnp.dot(h_ref[...], w_ref[...], preferred_element_type=F32)


def _norm_matmul_extra_kernel(x_ref, gain_ref, w_ref, w2_ref, o_ref, o2_ref, h_ref, *, n_silu_tiles):
    j = pl.program_id(1)

    @pl.when(j == 0)
    def _():
        _rmsnorm_rows(x_ref, gain_ref, h_ref)
        o2_ref[...] = jnp.dot(h_ref[...], w2_ref[...], preferred_element_type=F32)

    @pl.when(j < n_silu_tiles)
    def _():
        o_ref[...] = _silu(jnp.dot(h_ref[...], w_ref[...], preferred_element_type=F32))

    @pl.when(j >= n_silu_tiles)
    def _():
        o_ref[...] = jnp.dot(h_ref[...], w_ref[...], preferred_element_type=F32)


def _norm_matmul(x, gain, w, w_extra=None, *, n_cols, tm, tn, n_silu_cols=0):
    T, D = x.shape
    assert n_silu_cols % tn == 0
    N = n_cols
    grid = (T // tm, N // tn)
    in_specs = [
        pl.BlockSpec((tm, D), lambda i, j: (i, 0)),
        pl.BlockSpec((1, D), lambda i, j: (0, 0)),
        pl.BlockSpec((D, tn), lambda i, j: (0, j)),
    ]
    out_shape = [jax.ShapeDtypeStruct((T, N), F32)]
    out_specs = [pl.BlockSpec((tm, tn), lambda i, j: (i, j))]
    args = [x, gain.reshape(1, D), w]
    est = 2 * tm * D * 4 + tm * D * 2 + 2 * D * tn * 2 + 3 * tm * tn * 4
    if w_extra is None:
        body = _norm_matmul_kernel
    else:
        body = functools.partial(_norm_matmul_extra_kernel, n_silu_tiles=n_silu_cols // tn)
        n2 = w_extra.shape[1]
        in_specs.append(pl.BlockSpec((D, n2), lambda i, j: (0, 0)))
        out_shape.append(jax.ShapeDtypeStruct((T, n2), F32))
        out_specs.append(pl.BlockSpec((tm, n2), lambda i, j: (i, 0)))
        args.append(w_extra)
        est += 2 * D * n2 * 2 + 2 * tm * n2 * 4
    res = pl.pallas_call(
        body,
        grid=grid,
        in_specs=in_specs,
        out_specs=out_specs,
        out_shape=out_shape,
        scratch_shapes=[pltpu.VMEM((tm, D), BF16)],
        compiler_params=_cparams(("parallel", "arbitrary"), _vmem_limit(est)),
        name="norm_in_proj",
    )(*args)
    return res[0] if w_extra is None else res


def _proj_residual_kernel(*refs, n_lhs):
    lhs_refs = refs[:n_lhs]
    w_ref, x_ref, o_ref = refs[n_lhs:]
    acc = x_ref[...]
    k0 = 0
    for a_ref in lhs_refs:
        k1 = k0 + a_ref.shape[1]
        acc = acc + jnp.dot(a_ref[...], w_ref[k0:k1, :], preferred_element_type=F32)
        k0 = k1
    o_ref[...] = acc


def _proj_residual(lhs_list, w, x, *, tm, tn):
    T, N = x.shape
    K = w.shape[0]
    grid = (T // tm, N // tn)
    in_specs = [pl.BlockSpec((tm, a.shape[1]), lambda i, j: (i, 0)) for a in lhs_list]
    in_specs += [
        pl.BlockSpec((K, tn), lambda i, j: (0, j)),
        pl.BlockSpec((tm, tn), lambda i, j: (i, j)),
    ]
    est = 2 * tm * K * 2 + 2 * K * tn * 2 + 5 * tm * tn * 4
    return pl.pallas_call(
        functools.partial(_proj_residual_kernel, n_lhs=len(lhs_list)),
        grid=grid,
        in_specs=in_specs,
        out_specs=pl.BlockSpec((tm, tn), lambda i, j: (i, j)),
        out_shape=jax.ShapeDtypeStruct((T, N), F32),
        compiler_params=_cparams(("parallel", "arbitrary"), _vmem_limit(est)),
        name="out_proj_residual",
    )(*lhs_list, w, x)


def _ffn_kernel(x_ref, gain_ref, wg_ref, wu_ref, wd_ref, o_ref, h_ref):
    f = pl.program_id(1)

    @pl.when(f == 0)
    def _():
        _rmsnorm_rows(x_ref, gain_ref, h_ref)

    h = h_ref[...]
    g = jnp.dot(h, wg_ref[...], preferred_element_type=F32)
    u = jnp.dot(h, wu_ref[...], preferred_element_type=F32)
    a = (_silu(g) * u).astype(BF16)
    n_out = o_ref.shape[1]

    def down(c0):
        cols = slice(c0, min(c0 + FFN_OUT_COLS, n_out))
        return cols, jnp.dot(a, wd_ref[:, cols], preferred_element_type=F32)

    @pl.when(f == 0)
    def _():
        for c0 in range(0, n_out, FFN_OUT_COLS):
            cols, d = down(c0)
            o_ref[:, cols] = x_ref[:, cols] + d

    @pl.when(f != 0)
    def _():
        for c0 in range(0, n_out, FFN_OUT_COLS):
            cols, d = down(c0)
            o_ref[:, cols] += d


def _ffn(x, gain, wg, wu, wd, *, tm, tf):
    T, D = x.shape
    F = wg.shape[1]
    est = 4 * tm * D * 4 + tm * D * 2 + 3 * 2 * D * tf * 2 + 3 * tm * tf * 4 + tm * FFN_OUT_COLS * 4
    return pl.pallas_call(
        _ffn_kernel,
        grid=(T // tm, F // tf),
        in_specs=[
            pl.BlockSpec((tm, D), lambda i, f: (i, 0)),
            pl.BlockSpec((1, D), lambda i, f: (0, 0)),
            pl.BlockSpec((D, tf), lambda i, f: (0, f)),
            pl.BlockSpec((D, tf), lambda i, f: (0, f)),
            pl.BlockSpec((tf, D), lambda i, f: (f, 0)),
        ],
        out_specs=pl.BlockSpec((tm, D), lambda i, f: (i, 0)),
        out_shape=jax.ShapeDtypeStruct((T, D), F32),
        scratch_shapes=[pltpu.VMEM((tm, D), BF16)],
        compiler_params=_cparams(("parallel", "arbitrary"), _vmem_limit(est)),
        name="swiglu_ffn",
    )(x, gain.reshape(1, D), wg, wu, wd)


def _cumsum_rows(a):
    R, D = a.shape
    S8 = V7X_SUBLANES
    a3 = a.reshape(R // S8, S8, D)
    sub = lax.broadcasted_iota(jnp.int32, a3.shape, 1)
    shift = 1
    while shift < S8:
        a3 = a3 + jnp.where(sub >= shift, pltpu.roll(a3, shift, 1), 0.0)
        shift *= 2
    groups = []
    carry = None
    for g in range(R // S8):
        grp = a3[g]
        if carry is not None:
            grp = grp + carry
        groups.append(grp)
        carry = grp[S8 - 1:S8]
    return jnp.concatenate(groups, axis=0)


def _hgrn_level_table(n):
    t = np.arange(n)[:, None]
    s = np.arange(n)[None, :]
    lv = np.floor(np.log2(np.maximum(t ^ s, 1))).astype(np.int32)
    return jnp.asarray(np.where(t > s, lv, np.where(t == s, -1, -2)).astype(np.int32))


def _hgrn_kernel(q_ref, f_ref, i_ref, g_ref, lb_ref, nw_ref, lv_ref, o_ref, st_ref, *, slot):
    @pl.when(pl.program_id(2) == 0)
    def _():
        st_ref[...] = jnp.zeros_like(st_ref)

    for hh in range(q_ref.shape[1] // HG_DIM):
        cols = slice(hh * HG_DIM, (hh + 1) * HG_DIM)
        _hgrn_head(q_ref, f_ref, i_ref, g_ref, lb_ref, nw_ref, lv_ref, o_ref, st_ref.at[hh], cols, slot)


def _hgrn_head(q_ref, f_ref, i_ref, g_ref, lb_ref, nw_ref, lv_ref, o_ref, st_ref, cols, slot):
    C, D = q_ref.shape[0], HG_DIM
    H = lv_ref.shape[0]
    S8 = V7X_SUBLANES

    lbl = lb_ref[:, cols]
    ex = jnp.exp(lbl - jnp.max(lbl, axis=0, keepdims=True))
    lb = jnp.sum(ex[: slot + 1], axis=0, keepdims=True) / jnp.sum(ex, axis=0, keepdims=True)

    th = jnp.tanh(0.5 * f_ref[:, cols])
    q = _silu(q_ref[:, cols])
    log_f = jnp.log(lb + (1.0 - lb) * (0.5 + 0.5 * th))
    k = (1.0 - lb) * (0.5 - 0.5 * th)
    v = i_ref[:, cols]
    v_b = v.astype(BF16)

    b = _cumsum_rows(log_f) * LOG2_E
    row = lax.broadcasted_iota(jnp.int32, (C, D), 0)
    b8 = b.reshape(C // S8, S8, D)

    def bcast_row(r):
        return jnp.broadcast_to(b8[:, r:r + 1, :], b8.shape).reshape(C, D)

    lv = lv_ref[...]
    diag = jnp.sum(q * k, axis=1, keepdims=True)
    tiles = [jnp.where(lv == -1, diag[j * H:(j + 1) * H], 0.0) for j in range(C // H)]
    cross = None

    n_levels = int(math.log2(C))
    for lvl in range(n_levels):
        half = 1 << lvl
        blk = 2 * half
        is_q = (row & half) != 0
        if blk == 2:
            beta = jnp.where(is_q, pltpu.roll(b, 1, 0), b)
        elif blk < S8:
            beta = bcast_row(half - 1)
            for r0 in range(blk, S8, blk):
                beta = jnp.where((row & (S8 - 1)) >= r0, bcast_row(r0 + half - 1), beta)
        else:
            mid = b.reshape(C // blk, blk, D)[:, half - 1:half, :]
            beta = jnp.broadcast_to(mid, (C // blk, blk, D)).reshape(C, D)
        x = (jnp.where(is_q, q, k) * jnp.exp2(-jnp.abs(b - beta))).astype(BF16)
        if blk <= H:
            for j in range(C // H):
                xj = x[j * H:(j + 1) * H]
                p = lax.dot_general(xj, xj, (((1,), (1,)), ((), ())), preferred_element_type=F32)
                tiles[j] = jnp.where(lv == lvl, p, tiles[j])
        else:
            cross = lax.dot_general(x[H:], x[:H], (((1,), (1,)), ((), ())), preferred_element_type=F32)

    st = st_ref[...]
    o_top = jnp.dot(tiles[0].astype(BF16), v_b[:H], preferred_element_type=F32)
    o_bot = jnp.dot(jnp.concatenate([cross, tiles[1]], axis=1).astype(BF16), v_b, preferred_element_type=F32)
    qe = (q * jnp.exp2(b)).astype(BF16)
    o = jnp.concatenate([o_top, o_bot], axis=0) + lax.dot_general(
        qe, st.astype(BF16), (((1,), (1,)), ((), ())), preferred_element_type=F32)

    b_last = b[C - 1:C, :]
    k_dec = (k * jnp.exp2(b_last - b)).astype(BF16)
    v_t = v.T.astype(BF16)
    st_ref[...] = st * jnp.exp2(b_last) + jnp.dot(v_t, k_dec, preferred_element_type=F32)

    ms = jnp.mean(o * o, axis=-1, keepdims=True)
    o = o * lax.rsqrt(ms + EPS) * nw_ref[...] * _silu(g_ref[:, cols])
    o_ref[:, cols] = o.astype(o_ref.dtype)


def _hgrn(proj, lb_table, norm_w, *, batch, seq, n_heads, col0, slot, side=()):
    T = proj.shape[0]
    C = HGRN_CHUNK
    H = C // 2
    nc = seq // C
    hps = HGRN_HEADS_PER_STEP
    wb = hps * HG_DIM
    assert n_heads % hps == 0 and col0 % wb == 0 and (n_heads * HG_DIM) % wb == 0
    n_slots = lb_table.shape[0]

    def sec(s):
        cb = (col0 + s * n_heads * HG_DIM) // wb
        return pl.BlockSpec((C, wb), lambda b, h, c: (b * nc + c, cb + h))

    grid = (batch, n_heads // hps, nc)
    s_in, s_out, s_shapes, s_args = _side_casts(side, grid)
    res = pl.pallas_call(
        _with_side_casts(functools.partial(_hgrn_kernel, slot=slot), 7, 1, len(side)),
        grid=grid,
        in_specs=[
            sec(0), sec(1), sec(2), sec(3),
            pl.BlockSpec((n_slots, wb), lambda b, h, c: (0, h)),
            pl.BlockSpec((1, HG_DIM), lambda b, h, c: (0, 0)),
            pl.BlockSpec((H, H), lambda b, h, c: (0, 0)),
        ] + s_in,
        out_specs=[pl.BlockSpec((C, wb), lambda b, h, c: (b * nc + c, h))] + s_out,
        out_shape=[jax.ShapeDtypeStruct((T, n_heads * HG_DIM), BF16)] + s_shapes,
        scratch_shapes=[pltpu.VMEM((hps, HG_DIM, HG_DIM), F32)],
        compiler_params=_cparams(("arbitrary", "arbitrary", "arbitrary"), MIXER_VMEM_BYTES),
        name="hgrn2_scan",
    )(proj, proj, proj, proj, lb_table, norm_w.reshape(1, HG_DIM), _hgrn_level_table(H), *s_args)
    return res[0], res[1:]


def _moba_kernel(q_ref, k_ref, v_ref, qw_ref, kw_ref, o_ref):
    for hh in range(q_ref.shape[1] // MB_DIM):
        cols = slice(hh * MB_DIM, (hh + 1) * MB_DIM)
        _moba_head(q_ref.at[:, cols], k_ref.at[:, cols], v_ref.at[:, cols], qw_ref, kw_ref, o_ref.at[:, cols])


def _moba_head(q_ref, k_ref, v_ref, qw_ref, kw_ref, o_ref):
    S, D = q_ref.shape
    BLK = MB_BLOCK
    nb = S // BLK
    scale = D ** -0.5

    q = q_ref[...]
    k = k_ref[...]
    qn = q * lax.rsqrt(jnp.mean(q * q, axis=-1, keepdims=True) + EPS) * qw_ref[...]
    kn = k * lax.rsqrt(jnp.mean(k * k, axis=-1, keepdims=True) + EPS) * kw_ref[...]
    k_mean = jnp.mean(kn.reshape(nb, BLK, D), axis=1)
    qn_t = qn.T
    gate_t = jnp.dot(k_mean, qn_t, precision=lax.Precision.HIGHEST,
                     preferred_element_type=F32)
    qs_t = (qn_t * (scale * LOG2_E)).astype(BF16)
    kn_b = kn.astype(BF16)
    v_t = v_ref[...].T.astype(BF16)

    kk = lax.broadcasted_iota(jnp.int32, (BLK, BLK), 0)
    qq = lax.broadcasted_iota(jnp.int32, (BLK, BLK), 1)
    causal = kk <= qq

    for j in range(nb):
        cols = slice(j * BLK, (j + 1) * BLK)
        q_j = qs_t[:, cols]
        n_keep = min(MB_TOPK, j)
        selected = None
        if j > n_keep:
            g = [gate_t[m:m + 1, cols] for m in range(j)]
            selected = []
            for n in range(j):
                beaten = jnp.zeros((1, BLK), jnp.int32)
                for m in range(j):
                    if m < n:
                        beaten = beaten + (g[m] >= g[n]).astype(jnp.int32)
                    elif m > n:
                        beaten = beaten + (g[m] > g[n]).astype(jnp.int32)
                selected.append(beaten < n_keep)
        scores = []
        for n in range(j + 1):
            s = jnp.dot(kn_b[n * BLK:(n + 1) * BLK, :], q_j, preferred_element_type=F32)
            if n == j:
                s = jnp.where(causal, s, NEG_INF)
            elif selected is not None:
                s = jnp.where(selected[n], s, NEG_INF)
            scores.append(s)
        m_run = jnp.max(scores[0], axis=0, keepdims=True)
        for s in scores[1:]:
            m_run = jnp.maximum(m_run, jnp.max(s, axis=0, keepdims=True))
        denom = jnp.zeros((1, BLK), F32)
        acc = jnp.zeros((D, BLK), F32)
        for n, s in enumerate(scores):
            p = jnp.exp2(s - m_run)
            denom = denom + jnp.sum(p, axis=0, keepdims=True)
            acc = acc + jnp.dot(v_t[:, n * BLK:(n + 1) * BLK], p.astype(BF16), preferred_element_type=F32)
        o_ref[j * BLK:(j + 1) * BLK, :] = (acc / denom).T.astype(o_ref.dtype)


def _moba(proj, q_norm_w, k_norm_w, *, batch, seq, n_heads, col0, side=()):
    T = proj.shape[0]
    hps = MOBA_HEADS_PER_STEP
    wb = hps * MB_DIM
    assert n_heads % hps == 0 and col0 % wb == 0 and (n_heads * MB_DIM) % wb == 0

    def sec(s):
        cb = (col0 + s * n_heads * MB_DIM) // wb
        return pl.BlockSpec((seq, wb), lambda b, h: (b, cb + h))

    grid = (batch, n_heads // hps)
    s_in, s_out, s_shapes, s_args = _side_casts(side, grid)
    res = pl.pallas_call(
        _with_side_casts(_moba_kernel, 5, 1, len(side)),
        grid=grid,
        in_specs=[
            sec(0), sec(1), sec(2),
            pl.BlockSpec((1, MB_DIM), lambda b, h: (0, 0)),
            pl.BlockSpec((1, MB_DIM), lambda b, h: (0, 0)),
        ] + s_in,
        out_specs=[pl.BlockSpec((seq, wb), lambda b, h: (b, h))] + s_out,
        out_shape=[jax.ShapeDtypeStruct((T, n_heads * MB_DIM), BF16)] + s_shapes,
        compiler_params=_cparams(("arbitrary", "arbitrary"), MIXER_VMEM_BYTES),
        name="moba_attention",
    )(proj, proj, proj, q_norm_w.reshape(1, MB_DIM), k_norm_w.reshape(1, MB_DIM), *s_args)
    return res[0], res[1:]


def _causal_conv_silu(raw, tail, w_ref, bias_ref):
    K = SSM_CONV
    S8 = V7X_SUBLANES
    outs = []
    for t0 in range(0, raw.shape[0], CONV_ROW_TILE):
        cur = raw[t0:t0 + CONV_ROW_TILE]
        win = jnp.concatenate([tail if t0 == 0 else raw[t0 - S8:t0], cur], axis=0)
        acc = cur * w_ref[K - 1:K, :] + bias_ref[...]
        for d in range(1, K):
            acc = acc + pltpu.roll(win, d, 0)[S8:] * w_ref[K - 1 - d:K - d, :]
        outs.append(_silu(acc))
    return jnp.concatenate(outs, axis=0)


def _expand_heads(cols, n_heads, width, lane0=0):
    rows = cols.shape[0]
    per_vreg = V7X_LANES // width
    lane = lax.broadcasted_iota(jnp.int32, (rows, V7X_LANES), 1)
    blocks = []
    for h0 in range(lane0, lane0 + n_heads, per_vreg):
        blk = jnp.broadcast_to(cols[:, h0:h0 + 1], (rows, V7X_LANES))
        for k in range(1, per_vreg):
            nxt = jnp.broadcast_to(cols[:, h0 + k:h0 + k + 1], (rows, V7X_LANES))
            blk = jnp.where(lane >= k * width, nxt, blk)
        blocks.append(blk)
    return jnp.concatenate(blocks, axis=1)


def _ssd_kernel(z_ref, x_ref, bm_ref, cm_ref, dt_ref, cwx_ref, cwb_ref, cwc_ref, cbx_ref, cbb_ref, cbc_ref,
                dtb_ref, alog_ref, dsk_ref, nw_ref, o_ref, st_ref, tail_ref, cs2_ref, ecs_ref, wd_ref, key_ref,
                *, group_width):
    L = x_ref.shape[0]
    W, N = group_width, SSM_STATE
    gps = x_ref.shape[1] // W
    all_groups = gps == st_ref.shape[0]

    @pl.when(pl.program_id(2) == 0)
    def _():
        pre = dt_ref[...] + dtb_ref[...]
        dt_all = jnp.maximum(pre, 0.0) + jnp.log1p(jnp.exp(-jnp.abs(pre)))
        a_all = dt_all * (-jnp.exp(alog_ref[...]))
        cs_all = _cumsum_rows(a_all)
        cs2_all = cs_all * LOG2_E
        cs2_ref[...] = cs2_all
        ecs_ref[...] = jnp.exp(cs_all)
        wd_ref[...] = jnp.exp(cs_all[L - 1:L, :] - cs_all) * dt_all
        key_ref[...] = (cs2_all - jnp.log2(dt_all)).T

    for gg in range(gps):
        cw = slice(gg * W, (gg + 1) * W)
        cn = slice(gg * N, (gg + 1) * N)
        _ssd_group(z_ref.at[:, cw], x_ref.at[:, cw], bm_ref.at[:, cn], cm_ref.at[:, cn],
                   cwx_ref.at[:, cw], cwb_ref.at[:, cn], cwc_ref.at[:, cn],
                   cbx_ref.at[:, cw], cbb_ref.at[:, cn], cbc_ref.at[:, cn],
                   dsk_ref, nw_ref.at[:, cw], o_ref.at[:, cw], st_ref, tail_ref, cs2_ref, ecs_ref, wd_ref, key_ref,
                   gg if all_groups else pl.program_id(2) * gps + gg)


def _ssd_group(z_ref, x_ref, bm_ref, cm_ref, cwx_ref, cwb_ref, cwc_ref, cbx_ref, cbb_ref, cbc_ref,
               dsk_ref, nw_ref, o_ref, st_ref, tail_ref, cs2_ref, ecs_ref, wd_ref, key_ref, g):
    L, W = x_ref.shape
    P = SSM_HEADDIM
    hpg = W // P
    N = bm_ref.shape[1]

    @pl.when(pl.program_id(1) == 0)
    def _():
        st_ref[g] = jnp.zeros(st_ref.shape[1:], F32)
        tail_ref[g] = jnp.zeros(tail_ref.shape[1:], F32)

    x_raw = x_ref[...]
    b_raw = bm_ref[...]
    c_raw = cm_ref[...]
    tail = tail_ref[g]
    xs = _causal_conv_silu(x_raw, tail[:, :W], cwx_ref, cbx_ref)
    bm = _causal_conv_silu(b_raw, tail[:, W:W + N], cwb_ref, cbb_ref)
    cm = _causal_conv_silu(c_raw, tail[:, W + N:], cwc_ref, cbc_ref)
    tail_ref[g] = jnp.concatenate(
        [x_raw[L - V7X_SUBLANES:], b_raw[L - V7X_SUBLANES:], c_raw[L - V7X_SUBLANES:]], axis=1)

    if isinstance(g, int):
        head0 = lane0 = g * hpg
        cs2, ecs, wd, dsk = cs2_ref[...], ecs_ref[...], wd_ref[...], dsk_ref[...]
    else:
        head0, lane0 = pl.multiple_of(g * hpg, hpg), 0
        shift = (V7X_LANES - g * hpg) % V7X_LANES
        cs2, ecs, wd, dsk = (pltpu.roll(ref[...], shift, 1) for ref in (cs2_ref, ecs_ref, wd_ref, dsk_ref))
    key_t = key_ref[pl.ds(head0, hpg), :]

    xs_b = xs.astype(BF16)
    bm_b = bm.astype(BF16)
    cm_b = cm.astype(BF16)
    cb = lax.dot_general(cm_b, bm_b, (((1,), (1,)), ((), ())), preferred_element_type=F32)
    TL = V7X_LANES
    n_tiles = L // TL
    lower = (lax.broadcasted_iota(jnp.int32, (TL, TL), 0) >= lax.broadcasted_iota(jnp.int32, (TL, TL), 1))

    def head_rows(r, i):
        t0, s1 = i * TL, (i + 1) * TL
        seg = cs2[t0:s1, lane0 + r:lane0 + r + 1] - key_t[r:r + 1, :s1]
        diag = jnp.exp2(jnp.where(lower, seg[:, t0:], NEG_INF))
        lmat = diag if i == 0 else jnp.concatenate([jnp.exp2(seg[:, :t0]), diag], axis=1)
        return (cb[t0:s1, :s1] * lmat).astype(BF16)

    per_vreg = V7X_LANES // P
    lane_head = lax.broadcasted_iota(jnp.int32, (L, V7X_LANES), 1) // P
    zero = jnp.zeros((L, V7X_LANES), BF16)
    y_blocks = []
    for h0 in range(0, hpg, per_vreg):
        x_blk = xs_b[:, h0 * P:h0 * P + V7X_LANES]
        x_heads = [jnp.where(lane_head == k, x_blk, zero) for k in range(per_vreg)]
        y_rows = []
        for i in range(n_tiles):
            s1 = (i + 1) * TL
            lhs = jnp.concatenate([head_rows(h0 + k, i) for k in range(per_vreg)], axis=1)
            rhs = jnp.concatenate([xh[:s1] for xh in x_heads], axis=0)
            y_rows.append(jnp.dot(lhs, rhs, preferred_element_type=F32))
        y_blocks.append(jnp.concatenate(y_rows, axis=0))
    y = jnp.concatenate(y_blocks, axis=1)

    st = st_ref[g]
    y = y + jnp.dot(cm_b, st.astype(BF16), preferred_element_type=F32) * _expand_heads(ecs, hpg, P, lane0)
    y = y + _expand_heads(dsk, hpg, P, lane0) * xs

    xw = (xs * _expand_heads(wd, hpg, P, lane0)).astype(BF16)
    bm_t = bm.T.astype(BF16)
    st_ref[g] = (st * _expand_heads(ecs[L - 1:L, :], hpg, P, lane0)
                 + jnp.dot(bm_t, xw, preferred_element_type=F32))

    gy = y * z_ref[...]
    ms = jnp.mean(gy * gy, axis=-1, keepdims=True)
    o_ref[...] = (gy * lax.rsqrt(ms + EPS) * nw_ref[...]).astype(o_ref.dtype)


def _pad_lanes(a):
    return jnp.pad(a.reshape(1, -1), ((0, 0), (0, V7X_LANES - a.shape[0])))


def _ssd(zxbc, dt_raw, conv_w, conv_b, dt_bias, a_log, d_skip, norm_w, *, layer, batch, seq, side=()):
    T = zxbc.shape[0]
    G = SSM_GROUPS
    N = SSM_STATE
    heads = dt_bias.shape[1]
    assert heads <= V7X_LANES and dt_raw.shape[1] == V7X_LANES
    inner = heads * SSM_HEADDIM
    W = inner // G
    L = SSD_CHUNK
    nc = seq // L
    gps = SSD_GROUPS_PER_STEP
    assert G % gps == 0
    WB, NB = gps * W, gps * N
    xb, bb, cb_ = inner // WB, (2 * inner) // NB, (2 * inner + G * N) // NB
    wxb, wbb, wcb = 0, inner // NB, (inner + G * N) // NB

    dtb_p = _pad_lanes(dt_bias[layer])
    alog_p = _pad_lanes(a_log[layer])
    dsk_p = _pad_lanes(d_skip[layer])
    conv_b3 = conv_b.reshape(conv_b.shape[0], 1, -1)
    norm_w3 = norm_w.reshape(norm_w.shape[0], 1, -1)

    rows = lambda b, c, g: b * nc + c
    grid = (batch, nc, G // gps)
    s_in, s_out, s_shapes, s_args = _side_casts(side, grid)
    res = pl.pallas_call(
        _with_side_casts(functools.partial(_ssd_kernel, group_width=W), 15, 1, len(side)),
        grid=grid,
        in_specs=[
            pl.BlockSpec((L, WB), lambda b, c, g: (rows(b, c, g), g)),
            pl.BlockSpec((L, WB), lambda b, c, g: (rows(b, c, g), xb + g)),
            pl.BlockSpec((L, NB), lambda b, c, g: (rows(b, c, g), bb + g)),
            pl.BlockSpec((L, NB), lambda b, c, g: (rows(b, c, g), cb_ + g)),
            pl.BlockSpec((L, V7X_LANES), lambda b, c, g: (rows(b, c, g), 0)),
            pl.BlockSpec((None, SSM_CONV, WB), lambda b, c, g: (layer, 0, wxb + g)),
            pl.BlockSpec((None, SSM_CONV, NB), lambda b, c, g: (layer, 0, wbb + g)),
            pl.BlockSpec((None, SSM_CONV, NB), lambda b, c, g: (layer, 0, wcb + g)),
            pl.BlockSpec((None, 1, WB), lambda b, c, g: (layer, 0, wxb + g)),
            pl.BlockSpec((None, 1, NB), lambda b, c, g: (layer, 0, wbb + g)),
            pl.BlockSpec((None, 1, NB), lambda b, c, g: (layer, 0, wcb + g)),
            pl.BlockSpec((1, V7X_LANES), lambda b, c, g: (0, 0)),
            pl.BlockSpec((1, V7X_LANES), lambda b, c, g: (0, 0)),
            pl.BlockSpec((1, V7X_LANES), lambda b, c, g: (0, 0)),
            pl.BlockSpec((None, 1, WB), lambda b, c, g: (layer, 0, g)),
        ] + s_in,
        out_specs=[pl.BlockSpec((L, WB), lambda b, c, g: (rows(b, c, g), g))] + s_out,
        out_shape=[jax.ShapeDtypeStruct((T, inner), BF16)] + s_shapes,
        scratch_shapes=[
            pltpu.VMEM((G, N, W), F32),
            pltpu.VMEM((G, V7X_SUBLANES, W + 2 * N), F32),
            pltpu.VMEM((L, V7X_LANES), F32),
            pltpu.VMEM((L, V7X_LANES), F32),
            pltpu.VMEM((L, V7X_LANES), F32),
            pltpu.VMEM((V7X_LANES, L), F32),
        ],
        compiler_params=_cparams(("arbitrary", "arbitrary", "arbitrary"), MIXER_VMEM_BYTES),
        name="conv_ssd_scan",
    )(zxbc, zxbc, zxbc, zxbc, dt_raw, conv_w, conv_w, conv_w, conv_b3, conv_b3, conv_b3,
      dtb_p, alog_p, dsk_p, norm_w3, *s_args)
    return res[0], res[1:]


def kernel(x, ln_mix, ln_ffn, w_in_even, w_out_even, hgrn_lb, hgrn_norm, q_norm, k_norm, w_in_ssm, conv_w,
           conv_b, dt_bias, a_log, d_skip, ssm_norm, w_out_ssm, w_gate, w_up, w_down):
    batch, seq, d_model = x.shape
    depth = ln_mix.shape[0]
    T = batch * seq
    xf = x.reshape(T, d_model)

    hg_width = hgrn_lb.shape[1]
    hg_heads = hg_width // HG_DIM
    mb_width = (w_in_even.shape[2] - 4 * hg_width) // 3
    mb_heads = mb_width // MB_DIM
    ssm_heads = dt_bias.shape[1]
    ssm_main = w_in_ssm.shape[2] - ssm_heads

    def mixer_weights(layer):
        return (w_in_even, w_out_even, layer // 2) if layer % 2 == 0 else (w_in_ssm, w_out_ssm, layer // 2)

    tm = min(MATMUL_ROW_TILE, T)
    for layer in range(depth):
        w_in_l, w_out_l, idx = mixer_weights(layer)
        w_in_b = w_in_l[idx].astype(BF16)
        ffn_side = [(w_gate, layer), (w_up, layer), (w_down, layer)]
        out_side = [(w_out_l, idx)]
        if layer % 2 == 0:
            n_in = w_in_l.shape[2]
            proj = _norm_matmul(xf, ln_mix[layer], w_in_b, n_cols=n_in, tm=tm, tn=_col_tile(d_model, n_in))
            o_a, (wg_b, wu_b, wd_b) = _hgrn(proj, hgrn_lb, hgrn_norm[idx], batch=batch, seq=seq, n_heads=hg_heads,
                                            col0=0, slot=idx, side=ffn_side)
            o_b, (w_out_b,) = _moba(proj, q_norm[idx], k_norm[idx], batch=batch, seq=seq, n_heads=mb_heads,
                                    col0=4 * hg_width, side=out_side)
            mixed = [o_a, o_b]
        else:
            w_dt = jnp.pad(w_in_l[idx][:, ssm_main:], ((0, 0), (0, V7X_LANES - ssm_heads))).astype(BF16)
            zxbc, dt_raw = _norm_matmul(xf, ln_mix[layer], w_in_b, w_dt, n_cols=ssm_main, tm=tm,
                                        tn=_col_tile(d_model, ssm_main), n_silu_cols=ssm_heads * SSM_HEADDIM)
            gy, (wg_b, wu_b, wd_b, w_out_b) = _ssd(zxbc, dt_raw, conv_w, conv_b, dt_bias, a_log, d_skip, ssm_norm,
                                                   layer=idx, batch=batch, seq=seq, side=ffn_side + out_side)
            mixed = [gy]
        xf = _proj_residual(mixed, w_out_b, xf, tm=tm, tn=_col_tile(w_out_l.shape[1], d_model))
        xf = _ffn(xf, ln_ffn[layer], wg_b, wu_b, wd_b, tm=tm, tf=FFN_COL_TILE)
    return xf.reshape(batch, seq, d_model)
```
